```python
import jax, jax.numpy as jnp
from jax import lax
import numpy as np

D_MODEL = 1024
BATCH = 8
SEQ = 2048
DEPTH = 2

N_SB_HEADS = 8
SB_HEAD_DIM = 64
SB_WIDTH = N_SB_HEADS * SB_HEAD_DIM
Q_BLOCK = 128
POOL_WINDOWS = (2, 4, 8, 16)
POOL_GROUPS = len(POOL_WINDOWS)
POOL_GROUP_DIM = 128
POOL_WIDTH = POOL_GROUPS * POOL_GROUP_DIM
CONV_WIDTH = 512
CONV_TAPS = 31
SG_GROUPS = 4
SG_GROUP_DIM = 128
SG_WIDTH = SG_GROUPS * SG_GROUP_DIM
SG_CHUNK = 128
EVEN_IN = 3 * SB_WIDTH + POOL_WIDTH
ODD_IN = 2 * CONV_WIDTH + 2 * SG_WIDTH
MIX_WIDTH = 1024
D_FF = -(-(8 * D_MODEL) // (3 * 256)) * 256
PLE_DIM = 256
N_EVEN = (DEPTH + 1) // 2
N_ODD = DEPTH // 2
ALPHA = (2 * DEPTH) ** 0.25
BETA_INIT = (8 * DEPTH) ** -0.25
LN_EPS = 1e-5

kernel_name = "hybrid_stickbreak_pool_conformer_gmlp"


def layer_norm(x, g, b):
    xf = x.astype(jnp.float32)
    mu = jnp.mean(xf, axis=-1, keepdims=True)
    var = jnp.mean(jnp.square(xf - mu), axis=-1, keepdims=True)
    y = (xf - mu) * lax.rsqrt(var + LN_EPS)
    return (y * g.astype(jnp.float32) + b.astype(jnp.float32)).astype(x.dtype)


def stick_breaking_attention(q, k, v):
    S = q.shape[1]
    scale = SB_HEAD_DIM ** -0.5
    outs = []
    for i in range(S // Q_BLOCK):
        q0, q1 = i * Q_BLOCK, (i + 1) * Q_BLOCK
        qb = q[:, q0:q1]
        kb = k[:, :q1]
        vb = v[:, :q1]
        z = jnp.einsum('bqhd,bkhd->bhqk', qb, kb).astype(jnp.float32) * scale
        t_idx = q0 + jnp.arange(Q_BLOCK)[:, None]
        s_idx = jnp.arange(q1)[None, :]
        mask = s_idx < t_idx
        log_keep = jnp.where(mask, jax.nn.log_sigmoid(-z), 0.0)
        between = lax.cumsum(log_keep, axis=3, reverse=True) - log_keep
        w = jnp.where(mask, jnp.exp(jax.nn.log_sigmoid(z) + between), 0.0)
        outs.append(jnp.einsum('bhqk,bkhd->bqhd', w.astype(vb.dtype), vb))
    return jnp.concatenate(outs, axis=1)


def causal_window_mean(u, w):
    B, S, C = u.shape
    c = jnp.cumsum(u.astype(jnp.float32), axis=1)
    c_pad = jnp.concatenate([jnp.zeros((B, 1, C), jnp.float32), c], axis=1)
    hi = c_pad[:, 1:]
    lo = jnp.pad(c_pad[:, :S + 1 - w], ((0, 0), (w - 1, 0), (0, 0)))
    count = jnp.minimum(jnp.arange(1, S + 1), w).astype(jnp.float32)[None, :, None]
    return ((hi - lo) / count).astype(u.dtype)


def multiscale_pool(u, pool_w, pool_scale):
    B, S, _ = u.shape
    ug = u.reshape(B, S, POOL_GROUPS, POOL_GROUP_DIM)
    pooled = jnp.stack([causal_window_mean(ug[:, :, g], w) - ug[:, :, g]
                        for g, w in enumerate(POOL_WINDOWS)], axis=2)
    mixed = jnp.einsum('bsgc,gcd->bsgd', pooled, pool_w)
    return mixed.reshape(B, S, POOL_WIDTH) * pool_scale


def even_mixer(x, w_in, w_out, pool_w, pool_scale):
    B, S, _ = x.shape
    h = x @ w_in
    q = h[..., :SB_WIDTH].reshape(B, S, N_SB_HEADS, SB_HEAD_DIM)
    k = h[..., SB_WIDTH:2 * SB_WIDTH].reshape(B, S, N_SB_HEADS, SB_HEAD_DIM)
    v = h[..., 2 * SB_WIDTH:3 * SB_WIDTH].reshape(B, S, N_SB_HEADS, SB_HEAD_DIM)
    u = h[..., 3 * SB_WIDTH:]
    a = stick_breaking_attention(q, k, v).reshape(B, S, SB_WIDTH)
    b = multiscale_pool(u, pool_w, pool_scale)
    return jnp.concatenate([a, b], axis=-1) @ w_out


def conformer_conv(a, g, dw, ln_g, ln_b):
    h = a * jax.nn.sigmoid(g)
    h = lax.conv_general_dilated(h, dw[:, None, :], window_strides=(1,),
                                 padding=[(CONV_TAPS - 1, 0)],
                                 dimension_numbers=('NWC', 'WIO', 'NWC'),
                                 feature_group_count=CONV_WIDTH)
    return jax.nn.silu(layer_norm(h, ln_g, ln_b))


def chunked_spatial_gating(zc, ln_g, ln_b, sg_w, sg_b):
    B, S, _ = zc.shape
    z = jax.nn.gelu(zc)
    u, v = z[..., :SG_WIDTH], z[..., SG_WIDTH:]
    v = layer_norm(v, ln_g, ln_b)
    vc = v.reshape(B, S // SG_CHUNK, SG_CHUNK, SG_GROUPS, SG_GROUP_DIM)
    mask = jnp.tril(jnp.ones((SG_CHUNK, SG_CHUNK), sg_w.dtype))
    sv = jnp.einsum('gts,bcsgd->bctgd', sg_w * mask[None], vc)
    sv = sv + sg_b.T[None, None, :, :, None]
    return u * sv.reshape(B, S, SG_WIDTH)


def odd_mixer(x, w_in, w_out, conv_dw, conv_ln_g, conv_ln_b, sg_ln_g, sg_ln_b, sg_w, sg_b):
    h = x @ w_in
    a = h[..., :CONV_WIDTH]
    g = h[..., CONV_WIDTH:2 * CONV_WIDTH]
    zc = h[..., 2 * CONV_WIDTH:]
    c_out = conformer_conv(a, g, conv_dw, conv_ln_g, conv_ln_b)
    d_out = chunked_spatial_gating(zc, sg_ln_g, sg_ln_b, sg_w, sg_b)
    return jnp.concatenate([c_out, d_out], axis=-1) @ w_out


def swiglu(x, w_gate, w_up, w_down):
    return (jax.nn.silu(x @ w_gate) * (x @ w_up)) @ w_down


def setup_inputs(seed: int = 0) -> dict:
    key = jax.random.key(seed)
    ks = iter(jax.random.split(key, 40))
    nrm = lambda shape, s: jax.random.normal(next(ks), shape, jnp.float32) * s
    d = D_MODEL
    return {
        "x": nrm((BATCH, SEQ, d), 1.0),
        "p": nrm((DEPTH, BATCH, SEQ, PLE_DIM), 1.0),
        "even_w_in": nrm((N_EVEN, d, EVEN_IN), d ** -0.5),
        "even_w_out": nrm((N_EVEN, MIX_WIDTH, d), MIX_WIDTH ** -0.5 * BETA_INIT),
        "pool_w": nrm((N_EVEN, POOL_GROUPS, POOL_GROUP_DIM, POOL_GROUP_DIM), POOL_GROUP_DIM ** -0.5),
        "pool_scale": 1.0 + nrm((N_EVEN, POOL_WIDTH), 0.02),
        "odd_w_in": nrm((N_ODD, d, ODD_IN), d ** -0.5),
        "odd_w_out": nrm((N_ODD, MIX_WIDTH, d), MIX_WIDTH ** -0.5 * BETA_INIT),
        "conv_dw": nrm((N_ODD, CONV_TAPS, CONV_WIDTH), CONV_TAPS ** -0.5),
        "conv_ln_g": 1.0 + nrm((N_ODD, CONV_WIDTH), 0.02),
        "conv_ln_b": nrm((N_ODD, CONV_WIDTH), 0.02),
        "sg_ln_g": 1.0 + nrm((N_ODD, SG_WIDTH), 0.02),
        "sg_ln_b": nrm((N_ODD, SG_WIDTH), 0.02),
        "sg_w": nrm((N_ODD, SG_GROUPS, SG_CHUNK, SG_CHUNK), SG_CHUNK ** -0.5),
        "sg_b": 1.0 + nrm((N_ODD, SG_GROUPS, SG_CHUNK), 0.1),
        "ln_mix_g": 1.0 + nrm((DEPTH, d), 0.02),
        "ln_mix_b": nrm((DEPTH, d), 0.02),
        "ffn_w_gate": nrm((DEPTH, d, D_FF), d ** -0.5),
        "ffn_w_up": nrm((DEPTH, d, D_FF), d ** -0.5),
        "ffn_w_down": nrm((DEPTH, D_FF, d), D_FF ** -0.5 * BETA_INIT),
        "ln_ffn_g": 1.0 + nrm((DEPTH, d), 0.02),
        "ln_ffn_b": nrm((DEPTH, d), 0.02),
        "ple_w_proj": nrm((DEPTH, PLE_DIM, d), PLE_DIM ** -0.5),
        "ple_w_gate": nrm((DEPTH, d, d), d ** -0.5),
        "ple_b_gate": nrm((DEPTH, d), 0.02),
    }


def reference(x, p, even_w_in, even_w_out, pool_w, pool_scale,
              odd_w_in, odd_w_out, conv_dw, conv_ln_g, conv_ln_b,
              sg_ln_g, sg_ln_b, sg_w, sg_b,
              ln_mix_g, ln_mix_b, ffn_w_gate, ffn_w_up, ffn_w_down,
              ln_ffn_g, ln_ffn_b, ple_w_proj, ple_w_gate, ple_b_gate):
    for i in range(DEPTH):
        j = i // 2
        if i % 2 == 0:
            mix = even_mixer(x, even_w_in[j], even_w_out[j], pool_w[j], pool_scale[j])
        else:
            mix = odd_mixer(x, odd_w_in[j], odd_w_out[j], conv_dw[j], conv_ln_g[j], conv_ln_b[j],
                            sg_ln_g[j], sg_ln_b[j], sg_w[j], sg_b[j])
        x = layer_norm(ALPHA * x + mix, ln_mix_g[i], ln_mix_b[i])
        x = layer_norm(ALPHA * x + swiglu(x, ffn_w_gate[i], ffn_w_up[i], ffn_w_down[i]),
                       ln_ffn_g[i], ln_ffn_b[i])
        gate = jax.nn.sigmoid(x @ ple_w_gate[i] + ple_b_gate[i])
        x = x + gate * (p[i] @ ple_w_proj[i])
    return x
```

```python
import functools

import jax
import jax.numpy as jnp
from jax import lax
from jax.experimental import pallas as pl
from jax.experimental.pallas import tpu as pltpu

F32 = jnp.float32
BF16 = jnp.bfloat16

D_MODEL = 1024
N_SB_HEADS = 8
SB_HEAD_DIM = 64
SB_WIDTH = N_SB_HEADS * SB_HEAD_DIM
POOL_WINDOWS = (2, 4, 8, 16)
POOL_GROUP_DIM = 128
POOL_WIDTH = len(POOL_WINDOWS) * POOL_GROUP_DIM
CONV_WIDTH = 512
CONV_TAPS = 31
SG_GROUPS = 4
SG_GROUP_DIM = 128
SG_WIDTH = SG_GROUPS * SG_GROUP_DIM
SG_CHUNK = 128
MIX_WIDTH = 1024
D_FF = 2816
PLE_DIM = 256
DEPTH = 2
ALPHA = (2 * DEPTH) ** 0.25
LN_EPS = 1e-5

LANES = 128
MXU_DIM = 256
VMEM_LIMIT = 56 * 1024 * 1024

ROW_TILE = 512
HALO = 32
KEY_BLOCK = 128
FF_CHUNK = 256
CONV_ROWS = 64


def _layer_norm(y, g, b):
    mu = jnp.mean(y, axis=-1, keepdims=True)
    d = y - mu
    var = jnp.mean(d * d, axis=-1, keepdims=True)
    return d * lax.rsqrt(var + LN_EPS) * g + b


def _dot(a, b):
    return jnp.dot(a, b, preferred_element_type=F32)


def _const_spec(shape):
    zeros = (0,) * len(shape)
    return pl.BlockSpec(shape, lambda *_: zeros, pipeline_mode=pl.Buffered(1))


def _params(n_axes):
    return pltpu.CompilerParams(dimension_semantics=("arbitrary",) * n_axes,
                                vmem_limit_bytes=VMEM_LIMIT)


def _inproj0_kernel(x_ref, w_ref, q_ref, k_ref, v_ref, u_ref):
    xb = x_ref[...].astype(BF16)
    w = SB_WIDTH
    q_ref[...] = (_dot(xb, w_ref[:, 0:w]) * (SB_HEAD_DIM ** -0.5)).astype(BF16)
    k_ref[...] = _dot(xb, w_ref[:, w:2 * w]).astype(BF16)
    v_ref[...] = _dot(xb, w_ref[:, 2 * w:3 * w]).astype(BF16)
    u_ref[...] = _dot(xb, w_ref[:, 3 * w:])


def _inproj0(x2d, w_in):
    t = x2d.shape[0]
    row = lambda i: (i, 0)
    return pl.pallas_call(
        _inproj0_kernel,
        grid=(t // ROW_TILE,),
        in_specs=[pl.BlockSpec((ROW_TILE, D_MODEL), row), _const_spec(w_in.shape)],
        out_specs=[pl.BlockSpec((ROW_TILE, SB_WIDTH), row)] * 3 + [pl.BlockSpec((ROW_TILE, POOL_WIDTH), row)],
        out_shape=[jax.ShapeDtypeStruct((t, SB_WIDTH), BF16)] * 3 + [jax.ShapeDtypeStruct((t, POOL_WIDTH), F32)],
        compiler_params=_params(1),
        name="inproj0",
    )(x2d, w_in)


def _attn_kernel(q_ref, k_ref, v_ref, o_ref, kcat_ref, vcat_ref, acc_ref, carry_ref, *, seq):
    nb = seq // KEY_BLOCK
    kb_rows = 2 * KEY_BLOCK
    half = LANES // 2

    lane = lax.broadcasted_iota(jnp.int32, (seq, LANES), 1)
    is_a = lane < half
    k = k_ref[...]
    v = v_ref[...]
    zero = jnp.zeros_like(k)
    kcat_ref[:, 0:KEY_BLOCK, :] = jnp.where(is_a, k, zero).reshape(nb, KEY_BLOCK, LANES)
    kcat_ref[:, KEY_BLOCK:, :] = jnp.where(is_a, zero, k).reshape(nb, KEY_BLOCK, LANES)
    vcat_ref[:, 0:KEY_BLOCK, :] = jnp.where(is_a, v, zero).reshape(nb, KEY_BLOCK, LANES)
    vcat_ref[:, KEY_BLOCK:, :] = jnp.where(is_a, zero, v).reshape(nb, KEY_BLOCK, LANES)

    rj = lax.broadcasted_iota(jnp.int32, (kb_rows, kb_rows), 0)
    cs = lax.broadcasted_iota(jnp.int32, (kb_rows, kb_rows), 1)
    same_head = (rj >= KEY_BLOCK) == (cs >= KEY_BLOCK)
    suffix = jnp.where(same_head & (rj > cs), -1.0, 0.0).astype(BF16)

    tq = lax.broadcasted_iota(jnp.int32, (KEY_BLOCK, kb_rows), 0)
    sk = lax.broadcasted_iota(jnp.int32, (KEY_BLOCK, kb_rows), 1)
    sk = jnp.where(sk >= KEY_BLOCK, sk - KEY_BLOCK, sk)
    causal = sk < tq

    def visit(qb, kb, diagonal):
        kc = kcat_ref[kb]
        z = lax.dot_general(qb, kc, (((1,), (1,)), ((), ())), preferred_element_type=F32)
        sp = jnp.maximum(z, 0.0) + jnp.log(1.0 + jnp.exp(-jnp.abs(z)))
        log_beta = z - sp
        if diagonal:
            sp = jnp.where(causal, sp, 0.0)
        hi = sp.astype(BF16)
        lo = (sp - hi.astype(F32)).astype(BF16)
        between = _dot(hi, suffix) + _dot(lo, suffix)
        w = jnp.exp(log_beta + between + carry_ref[...])
        if diagonal:
            w = jnp.where(causal, w, 0.0)
        acc_ref[...] += _dot(w.astype(BF16), vcat_ref[kb])
        tot_a = jnp.sum(sp[:, 0:KEY_BLOCK], axis=1, keepdims=True)
        tot_b = jnp.sum(sp[:, KEY_BLOCK:], axis=1, keepdims=True)
        carry_ref[:, 0:KEY_BLOCK] = carry_ref[:, 0:KEY_BLOCK] - tot_a
        carry_ref[:, KEY_BLOCK:] = carry_ref[:, KEY_BLOCK:] - tot_b

    def q_body(i, _):
        q0 = pl.multiple_of(i * KEY_BLOCK, KEY_BLOCK)
        qb = q_ref[pl.ds(q0, KEY_BLOCK), :]
        acc_ref[...] = jnp.zeros_like(acc_ref)
        carry_ref[...] = jnp.zeros_like(carry_ref)
        visit(qb, i, True)

        def k_body(n, _):
            visit(qb, i - 1 - n, False)
            return 0

        lax.fori_loop(0, i, k_body, 0)
        o_ref[pl.ds(q0, KEY_BLOCK), :] = acc_ref[...].astype(o_ref.dtype)
        return 0

    lax.fori_loop(0, nb, q_body, 0)


def _attention(q, k, v):
    b, s, _ = q.shape
    nb = s // KEY_BLOCK
    spec = pl.BlockSpec((None, s, LANES), lambda bi, j: (bi, 0, j))
    return pl.pallas_call(
        functools.partial(_attn_kernel, seq=s),
        grid=(b, SB_WIDTH // LANES),
        in_specs=[spec, spec, spec],
        out_specs=spec,
        out_shape=jax.ShapeDtypeStruct((b, s, SB_WIDTH), BF16),
        scratch_shapes=[
            pltpu.VMEM((nb, 2 * KEY_BLOCK, LANES), BF16),
            pltpu.VMEM((nb, 2 * KEY_BLOCK, LANES), BF16),
            pltpu.VMEM((KEY_BLOCK, LANES), F32),
            pltpu.VMEM((KEY_BLOCK, 2 * KEY_BLOCK), F32),
        ],
        compiler_params=_params(2),
        name="stickbreak_attn",
    )(q, k, v)


def _mix0_kernel(a_ref, u_ref, uh_ref, x_ref, pw_ref, ps_ref, wo_ref, g_ref, b_ref, o_ref,
                 ubuf, s2, s4, s8):
    i = pl.program_id(1)
    ts = u_ref.shape[0]
    n = HALO + ts
    gd = POOL_GROUP_DIM
    ubuf[0:HALO, :] = jnp.where(i > 0, uh_ref[...], 0.0)
    ubuf[HALO:, :] = u_ref[...]
    s2[8:n, :] = ubuf[8:n, :] + ubuf[7:n - 1, :]
    s4[16:n, gd:] = s2[16:n, gd:] + s2[14:n - 2, gd:]
    s8[24:n, 2 * gd:] = s4[24:n, 2 * gd:] + s4[20:n - 4, 2 * gd:]
    s16 = s8[HALO:n, 3 * gd:] + s8[HALO - 8:n - 8, 3 * gd:]
    sums = (s2[HALO:n, 0:gd], s4[HALO:n, gd:2 * gd], s8[HALO:n, 2 * gd:3 * gd], s16)

    pos1 = (i * ts + 1 + lax.broadcasted_iota(jnp.int32, (ts, gd), 0)).astype(F32)
    mixed = []
    for g, w in enumerate(POOL_WINDOWS):
        cols = slice(g * gd, (g + 1) * gd)
        pooled = sums[g] / jnp.minimum(pos1, float(w)) - u_ref[:, cols]
        mixed.append((_dot(pooled.astype(BF16), pw_ref[g]) * ps_ref[:, cols]).astype(BF16))
    cat = jnp.concatenate([a_ref[...]] + mixed, axis=1)
    mix = _dot(cat, wo_ref[...])
    o_ref[...] = _layer_norm(ALPHA * x_ref[...] + mix, g_ref[...], b_ref[...])


def _mix0(a, u, x, pool_w, pool_scale, w_out, ln_g, ln_b):
    b, s, _ = x.shape
    ts = ROW_TILE
    tile = lambda bi, i: (bi, i, 0)
    halo = lambda bi, i: (bi, jnp.maximum(i * (ts // HALO) - 1, 0), 0)
    return pl.pallas_call(
        _mix0_kernel,
        grid=(b, s // ts),
        in_specs=[
            pl.BlockSpec((None, ts, SB_WIDTH), tile),
            pl.BlockSpec((None, ts, POOL_WIDTH), tile),
            pl.BlockSpec((None, HALO, POOL_WIDTH), halo),
            pl.BlockSpec((None, ts, D_MODEL), tile),
            _const_spec(pool_w.shape), _const_spec(pool_scale.shape), _const_spec(w_out.shape),
            _const_spec(ln_g.shape), _const_spec(ln_b.shape),
        ],
        out_specs=pl.BlockSpec((None, ts, D_MODEL), tile),
        out_shape=jax.ShapeDtypeStruct((b, s, D_MODEL), F32),
        scratch_shapes=[pltpu.VMEM((HALO + ts, POOL_WIDTH), F32)] * 4,
        compiler_params=_params(2),
        name="mix0",
    )(a, u, u, x, pool_w, pool_scale, w_out, ln_g, ln_b)


def _ffn_kernel(x_ref, p_ref, wg_ref, wu_ref, wd_ref, g_ref, b_ref, wpg_ref, bpg_ref, wpp_ref, o_ref,
                h_ref):
    x = x_ref[...]
    xb = x.astype(BF16)
    for c in range(D_FF // FF_CHUNK):
        cols = slice(c * FF_CHUNK, (c + 1) * FF_CHUNK)
        gate = _dot(xb, wg_ref[:, cols])
        up = _dot(xb, wu_ref[:, cols])
        h_ref[:, cols] = (jax.nn.silu(gate) * up).astype(BF16)
    y = _dot(h_ref[...], wd_ref[...])
    x2 = _layer_norm(ALPHA * x + y, g_ref[...], b_ref[...])
    ple_gate = jax.nn.sigmoid(_dot(x2.astype(BF16), wpg_ref[...]) + bpg_ref[...])
    o_ref[...] = x2 + ple_gate * _dot(p_ref[...].astype(BF16), wpp_ref[...])


def _ffn(x2d, p2d, w_gate, w_up, w_down, ln_g, ln_b, w_pg, b_pg, w_pp):
    t = x2d.shape[0]
    row = lambda i: (i, 0)
    consts = (w_gate, w_up, w_down, ln_g, ln_b, w_pg, b_pg, w_pp)
    return pl.pallas_call(
        _ffn_kernel,
        grid=(t // ROW_TILE,),
        in_specs=[pl.BlockSpec((ROW_TILE, D_MODEL), row), pl.BlockSpec((ROW_TILE, PLE_DIM), row)]
        + [_const_spec(c.shape) for c in consts],
        out_specs=pl.BlockSpec((ROW_TILE, D_MODEL), row),
        out_shape=jax.ShapeDtypeStruct((t, D_MODEL), F32),
        scratch_shapes=[pltpu.VMEM((ROW_TILE, D_FF), BF16)],
        compiler_params=_params(1),
        name="ffn_ple",
    )(x2d, p2d, *consts)


def _inproj1_kernel(x_ref, w_ref, g_ref, b_ref, hc_ref, u_ref, v_ref):
    xb = x_ref[...].astype(BF16)
    c = CONV_WIDTH
    a = _dot(xb, w_ref[:, 0:c])
    gl = _dot(xb, w_ref[:, c:2 * c])
    hc_ref[...] = a * jax.nn.sigmoid(gl)
    u_ref[...] = jax.nn.gelu(_dot(xb, w_ref[:, 2 * c:2 * c + SG_WIDTH]))
    zv = jax.nn.gelu(_dot(xb, w_ref[:, 2 * c + SG_WIDTH:]))
    v_ref[...] = _layer_norm(zv, g_ref[...], b_ref[...]).astype(BF16)


def _inproj1(x2d, w_in, sg_ln_g, sg_ln_b):
    t = x2d.shape[0]
    row = lambda i: (i, 0)
    return pl.pallas_call(
        _inproj1_kernel,
        grid=(t // ROW_TILE,),
        in_specs=[pl.BlockSpec((ROW_TILE, D_MODEL), row), _const_spec(w_in.shape),
                  _const_spec(sg_ln_g.shape), _const_spec(sg_ln_b.shape)],
        out_specs=[pl.BlockSpec((ROW_TILE, CONV_WIDTH), row)] * 3,
        out_shape=[jax.ShapeDtypeStruct((t, CONV_WIDTH), F32), jax.ShapeDtypeStruct((t, SG_WIDTH), F32),
                   jax.ShapeDtypeStruct((t, SG_WIDTH), BF16)],
        compiler_params=_params(1),
        name="inproj1",
    )(x2d, w_in, sg_ln_g, sg_ln_b)


def _mix1_kernel(hc_ref, hh_ref, u_ref, v_ref, x_ref, dw_ref, cg_ref, cb_ref, sw_ref, sb_ref, wo_ref,
                 g_ref, b_ref, o_ref, cbuf, conv_ref, cat_ref):
    i = pl.program_id(1)
    ts = hc_ref.shape[0]
    cbuf[0:HALO, :] = jnp.where(i > 0, hh_ref[...], 0.0)
    cbuf[HALO:, :] = hc_ref[...]

    first = HALO - (CONV_TAPS - 1)
    for lb in range(CONV_WIDTH // LANES):
        cols = slice(lb * LANES, (lb + 1) * LANES)
        for rb in range(ts // CONV_ROWS):
            r0 = rb * CONV_ROWS
            acc = jnp.zeros((CONV_ROWS, LANES), F32)
            for kk in range(CONV_TAPS):
                acc = acc + dw_ref[kk:kk + 1, cols] * cbuf[r0 + first + kk:r0 + first + kk + CONV_ROWS, cols]
            conv_ref[r0:r0 + CONV_ROWS, cols] = acc
    cat_ref[:, 0:CONV_WIDTH] = jax.nn.silu(_layer_norm(conv_ref[...], cg_ref[...], cb_ref[...])).astype(BF16)

    tt = lax.broadcasted_iota(jnp.int32, (SG_CHUNK, SG_CHUNK), 0)
    ss = lax.broadcasted_iota(jnp.int32, (SG_CHUNK, SG_CHUNK), 1)
    n_chunks = ts // SG_CHUNK
    for g in range(SG_GROUPS):
        cols = slice(g * SG_GROUP_DIM, (g + 1) * SG_GROUP_DIM)
        wm = jnp.where(ss <= tt, sw_ref[g], 0.0).astype(BF16)
        vcat = jnp.concatenate([v_ref[c * SG_CHUNK:(c + 1) * SG_CHUNK, cols] for c in range(n_chunks)], axis=1)
        sv = _dot(wm, vcat)
        for c in range(n_chunks):
            rows = slice(c * SG_CHUNK, (c + 1) * SG_CHUNK)
            gated = u_ref[rows, cols] * (sv[:, c * SG_GROUP_DIM:(c + 1) * SG_GROUP_DIM] + sb_ref[:, cols])
            cat_ref[rows, CONV_WIDTH + g * SG_GROUP_DIM:CONV_WIDTH + (g + 1) * SG_GROUP_DIM] = gated.astype(BF16)

    mix = _dot(cat_ref[...], wo_ref[...])
    o_ref[...] = _layer_norm(ALPHA * x_ref[...] + mix, g_ref[...], b_ref[...])


def _mix1(hc, u, v, x, conv_dw, conv_g, conv_b, sg_w, sg_bias, w_out, ln_g, ln_b):
    b, s, _ = x.shape
    ts = ROW_TILE
    tile = lambda bi, i: (bi, i, 0)
    halo = lambda bi, i: (bi, jnp.maximum(i * (ts // HALO) - 1, 0), 0)
    consts = (conv_dw, conv_g, conv_b, sg_w, sg_bias, w_out, ln_g, ln_b)
    return pl.pallas_call(
        _mix1_kernel,
        grid=(b, s // ts),
        in_specs=[
            pl.BlockSpec((None, ts, CONV_WIDTH), tile),
            pl.BlockSpec((None, HALO, CONV_WIDTH), halo),
            pl.BlockSpec((None, ts, SG_WIDTH), tile),
            pl.BlockSpec((None, ts, SG_WIDTH), tile),
            pl.BlockSpec((None, ts, D_MODEL), tile),
        ] + [_const_spec(c.shape) for c in consts],
        out_specs=pl.BlockSpec((None, ts, D_MODEL), tile),
        out_shape=jax.ShapeDtypeStruct((b, s, D_MODEL), F32),
        scratch_shapes=[pltpu.VMEM((HALO + ts, CONV_WIDTH), F32), pltpu.VMEM((ts, CONV_WIDTH), F32),
                        pltpu.VMEM((ts, MIX_WIDTH), BF16)],
        compiler_params=_params(2),
        name="mix1",
    )(hc, hc, u, v, x, *consts)


def _row(vec):
    return vec.reshape(1, -1)


def kernel(x, p, even_w_in, even_w_out, pool_w, pool_scale, odd_w_in, odd_w_out, conv_dw, conv_ln_g,
           conv_ln_b, sg_ln_g, sg_ln_b, sg_w, sg_b, ln_mix_g, ln_mix_b, ffn_w_gate, ffn_w_up, ffn_w_down,
           ln_ffn_g, ln_ffn_b, ple_w_proj, ple_w_gate, ple_b_gate):
    b, s, d = x.shape
    t = b * s
    bf = lambda w: w.astype(BF16)

    def ffn(xin, i):
        return _ffn(xin.reshape(t, d), p[i].reshape(t, PLE_DIM), bf(ffn_w_gate[i]), bf(ffn_w_up[i]),
                    bf(ffn_w_down[i]), _row(ln_ffn_g[i]), _row(ln_ffn_b[i]), bf(ple_w_gate[i]),
                    _row(ple_b_gate[i]), bf(ple_w_proj[i])).reshape(b, s, d)

    q, k, v, u = _inproj0(x.reshape(t, d), bf(even_w_in[0]))
    attn = _attention(q.reshape(b, s, SB_WIDTH), k.reshape(b, s, SB_WIDTH), v.reshape(b, s, SB_WIDTH))
    x = _mix0(attn, u.reshape(b, s, POOL_WIDTH), x, bf(pool_w[0]), _row(pool_scale[0]), bf(even_w_out[0]),
              _row(ln_mix_g[0]), _row(ln_mix_b[0]))
    x = ffn(x, 0)

    hc, su, sv = _inproj1(x.reshape(t, d), bf(odd_w_in[0]), _row(sg_ln_g[0]), _row(sg_ln_b[0]))
    sg_bias = jnp.repeat(sg_b[0].T, SG_GROUP_DIM, axis=1)
    x = _mix1(hc.reshape(b, s, CONV_WIDTH), su.reshape(b, s, SG_WIDTH), sv.reshape(b, s, SG_WIDTH), x,
              conv_dw[0], _row(conv_ln_g[0]), _row(conv_ln_b[0]), sg_w[0], sg_bias, bf(odd_w_out[0]),
              _row(ln_mix_g[1]), _row(ln_mix_b[1]))
    return ffn(x, 1)
```

```python
import functools

import jax
import jax.numpy as jnp
from jax import lax
from jax.experimental import pallas as pl
from jax.experimental.pallas import tpu as pltpu

F32 = jnp.float32
BF16 = jnp.bfloat16

D_MODEL = 1024
N_SB_HEADS = 8
SB_HEAD_DIM = 64
SB_WIDTH = N_SB_HEADS * SB_HEAD_DIM
POOL_WINDOWS = (2, 4, 8, 16)
POOL_GROUP_DIM = 128
POOL_WIDTH = len(POOL_WINDOWS) * POOL_GROUP_DIM
CONV_WIDTH = 512
CONV_TAPS = 31
SG_GROUPS = 4
SG_GROUP_DIM = 128
SG_WIDTH = SG_GROUPS * SG_GROUP_DIM
SG_CHUNK = 128
MIX_WIDTH = 1024
D_FF = 2816
PLE_DIM = 256
DEPTH = 2
ALPHA = (2 * DEPTH) ** 0.25
LN_EPS = 1e-5

LANES = 128
MXU_DIM = 256
VMEM_LIMIT = 56 * 1024 * 1024

ROW_TILE = 512
HALO = 32
KEY_BLOCK = 128
Q_TILE = 256
FF_CHUNK = 256
CONV_ROWS = 64


def _layer_norm(y, g, b):
    mu = jnp.mean(y, axis=-1, keepdims=True)
    d = y - mu
    var = jnp.mean(d * d, axis=-1, keepdims=True)
    return d * lax.rsqrt(var + LN_EPS) * g + b


def _dot(a, b):
    return jnp.dot(a, b, preferred_element_type=F32)


def _const_spec(shape):
    zeros = (0,) * len(shape)
    return pl.BlockSpec(shape, lambda *_: zeros, pipeline_mode=pl.Buffered(1))


def _params(n_axes):
    return pltpu.CompilerParams(dimension_semantics=("arbitrary",) * n_axes,
                                vmem_limit_bytes=VMEM_LIMIT)


def _inproj0_kernel(x_ref, w_ref, q_ref, k_ref, v_ref, u_ref):
    xb = x_ref[...].astype(BF16)
    w = SB_WIDTH
    q_ref[...] = (_dot(xb, w_ref[:, 0:w]) * (SB_HEAD_DIM ** -0.5)).astype(BF16)
    k_ref[...] = _dot(xb, w_ref[:, w:2 * w]).astype(BF16)
    v_ref[...] = _dot(xb, w_ref[:, 2 * w:3 * w]).astype(BF16)
    u_ref[...] = _dot(xb, w_ref[:, 3 * w:])


def _inproj0(x2d, w_in):
    t = x2d.shape[0]
    row = lambda i: (i, 0)
    return pl.pallas_call(
        _inproj0_kernel,
        grid=(t // ROW_TILE,),
        in_specs=[pl.BlockSpec((ROW_TILE, D_MODEL), row), _const_spec(w_in.shape)],
        out_specs=[pl.BlockSpec((ROW_TILE, SB_WIDTH), row)] * 3 + [pl.BlockSpec((ROW_TILE, POOL_WIDTH), row)],
        out_shape=[jax.ShapeDtypeStruct((t, SB_WIDTH), BF16)] * 3 + [jax.ShapeDtypeStruct((t, POOL_WIDTH), F32)],
        compiler_params=_params(1),
        name="inproj0",
    )(x2d, w_in)


def _attn_kernel(q_ref, k_ref, v_ref, o_ref, kt_ref, vcat_ref, sfx_ref, acc_ref, carry_ref, *, seq):
    nb = seq // KEY_BLOCK
    kb_cols = 2 * KEY_BLOCK
    half = LANES // 2
    per_tile = Q_TILE // KEY_BLOCK

    kt = k_ref[...].astype(F32).T
    dim = lax.broadcasted_iota(jnp.int32, kt.shape, 0)
    kt_a = jnp.where(dim < half, kt, 0.0).astype(BF16)
    kt_b = jnp.where(dim < half, 0.0, kt).astype(BF16)
    for kb in range(nb):
        kt_ref[kb, :, 0:KEY_BLOCK] = kt_a[:, kb * KEY_BLOCK:(kb + 1) * KEY_BLOCK]
        kt_ref[kb, :, KEY_BLOCK:] = kt_b[:, kb * KEY_BLOCK:(kb + 1) * KEY_BLOCK]
    lane = lax.broadcasted_iota(jnp.int32, (seq, LANES), 1)
    v = v_ref[...]
    zero = jnp.zeros_like(v)
    vcat_ref[:, 0:KEY_BLOCK, :] = jnp.where(lane < half, v, zero).reshape(nb, KEY_BLOCK, LANES)
    vcat_ref[:, KEY_BLOCK:, :] = jnp.where(lane < half, zero, v).reshape(nb, KEY_BLOCK, LANES)

    rj = lax.broadcasted_iota(jnp.int32, (2 * kb_cols, kb_cols), 0)
    cs = lax.broadcasted_iota(jnp.int32, (2 * kb_cols, kb_cols), 1)
    rj = jnp.where(rj >= kb_cols, rj - kb_cols, rj)
    in_suffix = ((rj < KEY_BLOCK) == (cs < KEY_BLOCK)) & (rj >= cs)
    sfx_ref[...] = jnp.where(in_suffix, -1.0, 0.0).astype(BF16)
    out_is_a = lax.broadcasted_iota(jnp.int32, (Q_TILE, LANES), 1) < half

    row = lax.broadcasted_iota(jnp.int32, (Q_TILE, kb_cols), 0)
    col = lax.broadcasted_iota(jnp.int32, (Q_TILE, kb_cols), 1)
    col = jnp.where(col >= KEY_BLOCK, col - KEY_BLOCK, col)

    def visit(qb, blocks):
        zs = [_dot(qb, kt_ref[kb]) for kb, _ in blocks]
        sps, masks = [], []
        for z, (_, key_offset) in zip(zs, blocks):
            sp = jnp.maximum(z, 0.0) + jnp.log(1.0 + jnp.exp(-jnp.abs(z)))
            causal = None if key_offset is None else col + key_offset < row
            masks.append(causal)
            sps.append(sp if causal is None else jnp.where(causal, sp, 0.0))
        his = [sp.astype(BF16) for sp in sps]
        los = [(sp - hi.astype(F32)).astype(BF16) for sp, hi in zip(sps, his)]
        sums = [_dot(jnp.concatenate([hi, lo], axis=1), sfx_ref[...]) for hi, lo in zip(his, los)]
        pvs = []
        for s, z, causal, (kb, _) in zip(sums, zs, masks, blocks):
            w = jnp.exp(z + s)
            if causal is not None:
                w = jnp.where(causal, w, 0.0)
            pvs.append(_dot(w.astype(BF16), vcat_ref[kb]))
        acc = acc_ref[...]
        carry = carry_ref[...]
        for pv, sp in zip(pvs, sps):
            acc = acc + jnp.exp(carry) * pv
            tot_a = jnp.sum(sp[:, 0:KEY_BLOCK], axis=1, keepdims=True)
            tot_b = jnp.sum(sp[:, KEY_BLOCK:], axis=1, keepdims=True)
            carry = carry - jnp.where(out_is_a, tot_a, tot_b)
        acc_ref[...] = acc
        carry_ref[...] = carry

    def q_body(i, _):
        q0 = pl.multiple_of(i * Q_TILE, Q_TILE)
        qb = q_ref[pl.ds(q0, Q_TILE), :]
        acc_ref[...] = jnp.zeros_like(acc_ref)
        carry_ref[...] = jnp.zeros_like(carry_ref)
        diagonal = [(i * per_tile + d, d * KEY_BLOCK) for d in reversed(range(per_tile))]
        odd = (i & 1) == 1

        @pl.when(odd)
        def _():
            visit(qb, diagonal + [(i * per_tile - 1 - d, None) for d in range(per_tile)])

        @pl.when(jnp.logical_not(odd))
        def _():
            visit(qb, diagonal)

        pairs = i // 2

        def k_body(n, _):
            first = (pairs - n) * 2 * per_tile - 1
            visit(qb, [(first - d, None) for d in range(2 * per_tile)])
            return 0

        lax.fori_loop(0, pairs, k_body, 0)
        o_ref[pl.ds(q0, Q_TILE), :] = acc_ref[...].astype(o_ref.dtype)
        return 0

    lax.fori_loop(0, seq // Q_TILE, q_body, 0)


def _attention(q, k, v):
    b, s, _ = q.shape
    nb = s // KEY_BLOCK
    spec = pl.BlockSpec((None, s, LANES), lambda bi, j: (bi, 0, j))
    return pl.pallas_call(
        functools.partial(_attn_kernel, seq=s),
        grid=(b, SB_WIDTH // LANES),
        in_specs=[spec, spec, spec],
        out_specs=spec,
        out_shape=jax.ShapeDtypeStruct((b, s, SB_WIDTH), BF16),
        scratch_shapes=[
            pltpu.VMEM((nb, LANES, 2 * KEY_BLOCK), BF16),
            pltpu.VMEM((nb, 2 * KEY_BLOCK, LANES), BF16),
            pltpu.VMEM((4 * KEY_BLOCK, 2 * KEY_BLOCK), BF16),
            pltpu.VMEM((Q_TILE, LANES), F32),
            pltpu.VMEM((Q_TILE, LANES), F32),
        ],
        compiler_params=_params(2),
        name="stickbreak_attn",
    )(q, k, v)


def _mix0_kernel(a_ref, u_ref, uh_ref, x_ref, pw_ref, ps_ref, wo_ref, g_ref, b_ref, o_ref,
                 ubuf, s2, s4, s8):
    i = pl.program_id(1)
    ts = u_ref.shape[0]
    n = HALO + ts
    gd = POOL_GROUP_DIM
    ubuf[0:HALO, :] = jnp.where(i > 0, uh_ref[...], 0.0)
    ubuf[HALO:, :] = u_ref[...]
    s2[8:n, :] = ubuf[8:n, :] + ubuf[7:n - 1, :]
    s4[16:n, gd:] = s2[16:n, gd:] + s2[14:n - 2, gd:]
    s8[24:n, 2 * gd:] = s4[24:n, 2 * gd:] + s4[20:n - 4, 2 * gd:]
    s16 = s8[HALO:n, 3 * gd:] + s8[HALO - 8:n - 8, 3 * gd:]
    sums = (s2[HALO:n, 0:gd], s4[HALO:n, gd:2 * gd], s8[HALO:n, 2 * gd:3 * gd], s16)

    pos1 = (i * ts + 1 + lax.broadcasted_iota(jnp.int32, (ts, gd), 0)).astype(F32)
    mixed = []
    for g, w in enumerate(POOL_WINDOWS):
        cols = slice(g * gd, (g + 1) * gd)
        pooled = sums[g] / jnp.minimum(pos1, float(w)) - u_ref[:, cols]
        mixed.append((_dot(pooled.astype(BF16), pw_ref[g]) * ps_ref[:, cols]).astype(BF16))
    cat = jnp.concatenate([a_ref[...]] + mixed, axis=1)
    mix = _dot(cat, wo_ref[...])
    o_ref[...] = _layer_norm(ALPHA * x_ref[...] + mix, g_ref[...], b_ref[...])


def _mix0(a, u, x, pool_w, pool_scale, w_out, ln_g, ln_b):
    b, s, _ = x.shape
    ts = ROW_TILE
    tile = lambda bi, i: (bi, i, 0)
    halo = lambda bi, i: (bi, jnp.maximum(i * (ts // HALO) - 1, 0), 0)
    return pl.pallas_call(
        _mix0_kernel,
        grid=(b, s // ts),
        in_specs=[
            pl.BlockSpec((None, ts, SB_WIDTH), tile),
            pl.BlockSpec((None, ts, POOL_WIDTH), tile),
            pl.BlockSpec((None, HALO, POOL_WIDTH), halo),
            pl.BlockSpec((None, ts, D_MODEL), tile),
            _const_spec(pool_w.shape), _const_spec(pool_scale.shape), _const_spec(w_out.shape),
            _const_spec(ln_g.shape), _const_spec(ln_b.shape),
        ],
        out_specs=pl.BlockSpec((None, ts, D_MODEL), tile),
        out_shape=jax.ShapeDtypeStruct((b, s, D_MODEL), F32),
        scratch_shapes=[pltpu.VMEM((HALO + ts, POOL_WIDTH), F32)] * 4,
        compiler_params=_params(2),
        name="mix0",
    )(a, u, u, x, pool_w, pool_scale, w_out, ln_g, ln_b)


def _ffn_kernel(x_ref, p_ref, wg_ref, wu_ref, wd_ref, g_ref, b_ref, wpg_ref, bpg_ref, wpp_ref, o_ref,
                h_ref):
    x = x_ref[...]
    xb = x.astype(BF16)
    for c in range(D_FF // FF_CHUNK):
        cols = slice(c * FF_CHUNK, (c + 1) * FF_CHUNK)
        gate = _dot(xb, wg_ref[:, cols])
        up = _dot(xb, wu_ref[:, cols])
        h_ref[:, cols] = (jax.nn.silu(gate) * up).astype(BF16)
    y = _dot(h_ref[...], wd_ref[...])
    x2 = _layer_norm(ALPHA * x + y, g_ref[...], b_ref[...])
    ple_gate = jax.nn.sigmoid(_dot(x2.astype(BF16), wpg_ref[...]) + bpg_ref[...])
    o_ref[...] = x2 + ple_gate * _dot(p_ref[...].astype(BF16), wpp_ref[...])


def _ffn(x2d, p2d, w_gate, w_up, w_down, ln_g, ln_b, w_pg, b_pg, w_pp):
    t = x2d.shape[0]
    row = lambda i: (i, 0)
    consts = (w_gate, w_up, w_down, ln_g, ln_b, w_pg, b_pg, w_pp)
    return pl.pallas_call(
        _ffn_kernel,
        grid=(t // ROW_TILE,),
        in_specs=[pl.BlockSpec((ROW_TILE, D_MODEL), row), pl.BlockSpec((ROW_TILE, PLE_DIM), row)]
        + [_const_spec(c.shape) for c in consts],
        out_specs=pl.BlockSpec((ROW_TILE, D_MODEL), row),
        out_shape=jax.ShapeDtypeStruct((t, D_MODEL), F32),
        scratch_shapes=[pltpu.VMEM((ROW_TILE, D_FF), BF16)],
        compiler_params=_params(1),
        name="ffn_ple",
    )(x2d, p2d, *consts)


def _inproj1_kernel(x_ref, w_ref, g_ref, b_ref, hc_ref, u_ref, v_ref):
    xb = x_ref[...].astype(BF16)
    c = CONV_WIDTH
    a = _dot(xb, w_ref[:, 0:c])
    gl = _dot(xb, w_ref[:, c:2 * c])
    hc_ref[...] = a * jax.nn.sigmoid(gl)
    u_ref[...] = jax.nn.gelu(_dot(xb, w_ref[:, 2 * c:2 * c + SG_WIDTH]))
    zv = jax.nn.gelu(_dot(xb, w_ref[:, 2 * c + SG_WIDTH:]))
    v_ref[...] = _layer_norm(zv, g_ref[...], b_ref[...]).astype(BF16)


def _inproj1(x2d, w_in, sg_ln_g, sg_ln_b):
    t = x2d.shape[0]
    row = lambda i: (i, 0)
    return pl.pallas_call(
        _inproj1_kernel,
        grid=(t // ROW_TILE,),
        in_specs=[pl.BlockSpec((ROW_TILE, D_MODEL), row), _const_spec(w_in.shape),
                  _const_spec(sg_ln_g.shape), _const_spec(sg_ln_b.shape)],
        out_specs=[pl.BlockSpec((ROW_TILE, CONV_WIDTH), row)] * 3,
        out_shape=[jax.ShapeDtypeStruct((t, CONV_WIDTH), F32), jax.ShapeDtypeStruct((t, SG_WIDTH), F32),
                   jax.ShapeDtypeStruct((t, SG_WIDTH), BF16)],
        compiler_params=_params(1),
        name="inproj1",
    )(x2d, w_in, sg_ln_g, sg_ln_b)


def _mix1_kernel(hc_ref, hh_ref, u_ref, v_ref, x_ref, dw_ref, cg_ref, cb_ref, sw_ref, sb_ref, wo_ref,
                 g_ref, b_ref, o_ref, cbuf, conv_ref, cat_ref):
    i = pl.program_id(1)
    ts = hc_ref.shape[0]
    cbuf[0:HALO, :] = jnp.where(i > 0, hh_ref[...], 0.0)
    cbuf[HALO:, :] = hc_ref[...]

    first = HALO - (CONV_TAPS - 1)
    for lb in range(CONV_WIDTH // LANES):
        cols = slice(lb * LANES, (lb + 1) * LANES)
        for rb in range(ts // CONV_ROWS):
            r0 = rb * CONV_ROWS
            acc = jnp.zeros((CONV_ROWS, LANES), F32)
            for kk in range(CONV_TAPS):
                acc = acc + dw_ref[kk:kk + 1, cols] * cbuf[r0 + first + kk:r0 + first + kk + CONV_ROWS, cols]
            conv_ref[r0:r0 + CONV_ROWS, cols] = acc
    cat_ref[:, 0:CONV_WIDTH] = jax.nn.silu(_layer_norm(conv_ref[...], cg_ref[...], cb_ref[...])).astype(BF16)

    tt = lax.broadcasted_iota(jnp.int32, (SG_CHUNK, SG_CHUNK), 0)
    ss = lax.broadcasted_iota(jnp.int32, (SG_CHUNK, SG_CHUNK), 1)
    n_chunks = ts // SG_CHUNK
    for g in range(SG_GROUPS):
        cols = slice(g * SG_GROUP_DIM, (g + 1) * SG_GROUP_DIM)
        wm = jnp.where(ss <= tt, sw_ref[g], 0.0).astype(BF16)
        vcat = jnp.concatenate([v_ref[c * SG_CHUNK:(c + 1) * SG_CHUNK, cols] for c in range(n_chunks)], axis=1)
        sv = _dot(wm, vcat)
        for c in range(n_chunks):
            rows = slice(c * SG_CHUNK, (c + 1) * SG_CHUNK)
            gated = u_ref[rows, cols] * (sv[:, c * SG_GROUP_DIM:(c + 1) * SG_GROUP_DIM] + sb_ref[:, cols])
            cat_ref[rows, CONV_WIDTH + g * SG_GROUP_DIM:CONV_WIDTH + (g + 1) * SG_GROUP_DIM] = gated.astype(BF16)

    mix = _dot(cat_ref[...], wo_ref[...])
    o_ref[...] = _layer_norm(ALPHA * x_ref[...] + mix, g_ref[...], b_ref[...])


def _mix1(hc, u, v, x, conv_dw, conv_g, conv_b, sg_w, sg_bias, w_out, ln_g, ln_b):
    b, s, _ = x.shape
    ts = ROW_TILE
    tile = lambda bi, i: (bi, i, 0)
    halo = lambda bi, i: (bi, jnp.maximum(i * (ts // HALO) - 1, 0), 0)
    consts = (conv_dw, conv_g, conv_b, sg_w, sg_bias, w_out, ln_g, ln_b)
    return pl.pallas_call(
        _mix1_kernel,
        grid=(b, s // ts),
        in_specs=[
            pl.BlockSpec((None, ts, CONV_WIDTH), tile),
            pl.BlockSpec((None, HALO, CONV_WIDTH), halo),
            pl.BlockSpec((None, ts, SG_WIDTH), tile),
            pl.BlockSpec((None, ts, SG_WIDTH), tile),
            pl.BlockSpec((None, ts, D_MODEL), tile),
        ] + [_const_spec(c.shape) for c in consts],
        out_specs=pl.BlockSpec((None, ts, D_MODEL), tile),
        out_shape=jax.ShapeDtypeStruct((b, s, D_MODEL), F32),
        scratch_shapes=[pltpu.VMEM((HALO + ts, CONV_WIDTH), F32), pltpu.VMEM((ts, CONV_WIDTH), F32),
                        pltpu.VMEM((ts, MIX_WIDTH), BF16)],
        compiler_params=_params(2),
        name="mix1",
    )(hc, hc, u, v, x, *consts)


def _row(vec):
    return vec.reshape(1, -1)


def kernel(x, p, even_w_in, even_w_out, pool_w, pool_scale, odd_w_in, odd_w_out, conv_dw, conv_ln_g,
           conv_ln_b, sg_ln_g, sg_ln_b, sg_w, sg_b, ln_mix_g, ln_mix_b, ffn_w_gate, ffn_w_up, ffn_w_down,
           ln_ffn_g, ln_ffn_b, ple_w_proj, ple_w_gate, ple_b_gate):
    b, s, d = x.shape
    t = b * s
    bf = lambda w: w.astype(BF16)

    def ffn(xin, i):
        return _ffn(xin.reshape(t, d), p[i].reshape(t, PLE_DIM), bf(ffn_w_gate[i]), bf(ffn_w_up[i]),
                    bf(ffn_w_down[i]), _row(ln_ffn_g[i]), _row(ln_ffn_b[i]), bf(ple_w_gate[i]),
                    _row(ple_b_gate[i]), bf(ple_w_proj[i])).reshape(b, s, d)

    q, k, v, u = _inproj0(x.reshape(t, d), bf(even_w_in[0]))
    attn = _attention(q.reshape(b, s, SB_WIDTH), k.reshape(b, s, SB_WIDTH), v.reshape(b, s, SB_WIDTH))
    x = _mix0(attn, u.reshape(b, s, POOL_WIDTH), x, bf(pool_w[0]), _row(pool_scale[0]), bf(even_w_out[0]),
              _row(ln_mix_g[0]), _row(ln_mix_b[0]))
    x = ffn(x, 0)

    hc, su, sv = _inproj1(x.reshape(t, d), bf(odd_w_in[0]), _row(sg_ln_g[0]), _row(sg_ln_b[0]))
    sg_bias = jnp.repeat(sg_b[0].T, SG_GROUP_DIM, axis=1)
    x = _mix1(hc.reshape(b, s, CONV_WIDTH), su.reshape(b, s, SG_WIDTH), sv.reshape(b, s, SG_WIDTH), x,
              conv_dw[0], _row(conv_ln_g[0]), _row(conv_ln_b[0]), sg_w[0], sg_bias, bf(odd_w_out[0]),
              _row(ln_mix_g[1]), _row(ln_mix_b[1]))
    return ffn(x, 1)
```

```python
import functools

import jax
import jax.numpy as jnp
from jax import lax
from jax.experimental import pallas as pl
from jax.experimental.pallas import tpu as pltpu

F32 = jnp.float32
BF16 = jnp.bfloat16

D_MODEL = 1024
N_SB_HEADS = 8
SB_HEAD_DIM = 64
SB_WIDTH = N_SB_HEADS * SB_HEAD_DIM
POOL_WINDOWS = (2, 4, 8, 16)
POOL_GROUP_DIM = 128
POOL_WIDTH = len(POOL_WINDOWS) * POOL_GROUP_DIM
CONV_WIDTH = 512
CONV_TAPS = 31
SG_GROUPS = 4
SG_GROUP_DIM = 128
SG_WIDTH = SG_GROUPS * SG_GROUP_DIM
SG_CHUNK = 128
MIX_WIDTH = 1024
D_FF = 2816
PLE_DIM = 256
DEPTH = 2
ALPHA = (2 * DEPTH) ** 0.25
LN_EPS = 1e-5

LANES = 128
SUBLANES = 8
MXU_DIM = 256
VMEM_LIMIT = 56 * 1024 * 1024

ROW_TILE = 512
HALO = 32
KEY_BLOCK = 128
Q_TILE = 256
EXP_UNDERFLOW_LOG = -104.0
FF_CHUNK = 256
CONV_ROWS = 64


def _layer_norm(y, g, b):
    mu = jnp.mean(y, axis=-1, keepdims=True)
    d = y - mu
    var = jnp.mean(d * d, axis=-1, keepdims=True)
    return d * lax.rsqrt(var + LN_EPS) * g + b


def _dot(a, b):
    return jnp.dot(a, b, preferred_element_type=F32)


def _const_spec(shape):
    zeros = (0,) * len(shape)
    return pl.BlockSpec(shape, lambda *_: zeros, pipeline_mode=pl.Buffered(1))


def _params(n_axes):
    return pltpu.CompilerParams(dimension_semantics=("arbitrary",) * n_axes,
                                vmem_limit_bytes=VMEM_LIMIT)


def _inproj0_kernel(x_ref, w_ref, q_ref, k_ref, v_ref, u_ref):
    xb = x_ref[...].astype(BF16)
    w = SB_WIDTH
    q_ref[...] = (_dot(xb, w_ref[:, 0:w]) * (SB_HEAD_DIM ** -0.5)).astype(BF16)
    k_ref[...] = _dot(xb, w_ref[:, w:2 * w]).astype(BF16)
    v_ref[...] = _dot(xb, w_ref[:, 2 * w:3 * w]).astype(BF16)
    u_ref[...] = _dot(xb, w_ref[:, 3 * w:])


def _inproj0(x2d, w_in):
    t = x2d.shape[0]
    row = lambda i: (i, 0)
    return pl.pallas_call(
        _inproj0_kernel,
        grid=(t // ROW_TILE,),
        in_specs=[pl.BlockSpec((ROW_TILE, D_MODEL), row), _const_spec(w_in.shape)],
        out_specs=[pl.BlockSpec((ROW_TILE, SB_WIDTH), row)] * 3 + [pl.BlockSpec((ROW_TILE, POOL_WIDTH), row)],
        out_shape=[jax.ShapeDtypeStruct((t, SB_WIDTH), BF16)] * 3 + [jax.ShapeDtypeStruct((t, POOL_WIDTH), F32)],
        compiler_params=_params(1),
        name="inproj0",
    )(x2d, w_in)


def _attn_kernel(q_ref, k_ref, v_ref, o_ref, kt_ref, vcat_ref, sfx_ref, acc_ref, carry_ref, *, seq):
    nb = seq // KEY_BLOCK
    kb_cols = 2 * KEY_BLOCK
    half = LANES // 2
    per_tile = Q_TILE // KEY_BLOCK

    kt = k_ref[...].astype(F32).T
    dim = lax.broadcasted_iota(jnp.int32, kt.shape, 0)
    kt_a = jnp.where(dim < half, kt, 0.0).astype(BF16)
    kt_b = jnp.where(dim < half, 0.0, kt).astype(BF16)
    for kb in range(nb):
        kt_ref[kb, :, 0:KEY_BLOCK] = kt_a[:, kb * KEY_BLOCK:(kb + 1) * KEY_BLOCK]
        kt_ref[kb, :, KEY_BLOCK:] = kt_b[:, kb * KEY_BLOCK:(kb + 1) * KEY_BLOCK]
    lane = lax.broadcasted_iota(jnp.int32, (seq, LANES), 1)
    v = v_ref[...]
    zero = jnp.zeros_like(v)
    vcat_ref[:, 0:KEY_BLOCK, :] = jnp.where(lane < half, v, zero).reshape(nb, KEY_BLOCK, LANES)
    vcat_ref[:, KEY_BLOCK:, :] = jnp.where(lane < half, zero, v).reshape(nb, KEY_BLOCK, LANES)

    rj = lax.broadcasted_iota(jnp.int32, (2 * kb_cols, kb_cols), 0)
    cs = lax.broadcasted_iota(jnp.int32, (2 * kb_cols, kb_cols), 1)
    rj = jnp.where(rj >= kb_cols, rj - kb_cols, rj)
    in_suffix = ((rj < KEY_BLOCK) == (cs < KEY_BLOCK)) & (rj >= cs)
    sfx_ref[...] = jnp.where(in_suffix, -1.0, 0.0).astype(BF16)
    out_is_a = lax.broadcasted_iota(jnp.int32, (Q_TILE, LANES), 1) < half

    row = lax.broadcasted_iota(jnp.int32, (Q_TILE, kb_cols), 0)
    col = lax.broadcasted_iota(jnp.int32, (Q_TILE, kb_cols), 1)
    col = jnp.where(col >= KEY_BLOCK, col - KEY_BLOCK, col)

    def visit(qb, blocks):
        zs = [_dot(qb, kt_ref[kb]) for kb, _ in blocks]
        sps, masks = [], []
        for z, (_, key_offset) in zip(zs, blocks):
            sp = jnp.maximum(z, 0.0) + jnp.log(1.0 + jnp.exp(-jnp.abs(z)))
            causal = None if key_offset is None else col + key_offset < row
            masks.append(causal)
            sps.append(sp if causal is None else jnp.where(causal, sp, 0.0))
        his = [sp.astype(BF16) for sp in sps]
        los = [(sp - hi.astype(F32)).astype(BF16) for sp, hi in zip(sps, his)]
        sums = [_dot(jnp.concatenate([hi, lo], axis=1), sfx_ref[...]) for hi, lo in zip(his, los)]
        pvs = []
        for s, z, causal, (kb, _) in zip(sums, zs, masks, blocks):
            w = jnp.exp(z + s)
            if causal is not None:
                w = jnp.where(causal, w, 0.0)
            pvs.append(_dot(w.astype(BF16), vcat_ref[kb]))
        acc = acc_ref[...]
        carry = carry_ref[...]
        for pv, sp in zip(pvs, sps):
            acc = acc + jnp.exp(carry) * pv
            tot_a = jnp.sum(sp[:, 0:KEY_BLOCK], axis=1, keepdims=True)
            tot_b = jnp.sum(sp[:, KEY_BLOCK:], axis=1, keepdims=True)
            carry = carry - jnp.where(out_is_a, tot_a, tot_b)
        acc_ref[...] = acc
        carry_ref[...] = carry

    def q_body(i, _):
        q0 = pl.multiple_of(i * Q_TILE, Q_TILE)
        qb = q_ref[pl.ds(q0, Q_TILE), :]
        acc_ref[...] = jnp.zeros_like(acc_ref)
        carry_ref[...] = jnp.zeros_like(carry_ref)
        diagonal = [(i * per_tile + d, d * KEY_BLOCK) for d in reversed(range(per_tile))]

        @pl.when(i == 0)
        def _():
            visit(qb, diagonal)

        @pl.when(i > 0)
        def _():
            visit(qb, diagonal + [(i * per_tile - 1 - d, None) for d in range(per_tile)])

        def max_carry():
            return jnp.max(carry_ref[...])

        left = jnp.maximum(i - 1, 0) * per_tile
        trips = left // (2 * per_tile)

        def k_cond(state):
            n, top = state
            return (n < trips) & (top > EXP_UNDERFLOW_LOG)

        def k_body(state):
            n, _ = state
            first = left - 1 - n * 2 * per_tile
            visit(qb, [(first - d, None) for d in range(2 * per_tile)])
            return n + 1, max_carry()

        n_done, top = lax.while_loop(k_cond, k_body, (0, max_carry()))

        @pl.when((n_done == trips) & (left - trips * 2 * per_tile > 0) & (top > EXP_UNDERFLOW_LOG))
        def _():
            visit(qb, [(per_tile - 1 - d, None) for d in range(per_tile)])

        o_ref[pl.ds(q0, Q_TILE), :] = acc_ref[...].astype(o_ref.dtype)
        return 0

    lax.fori_loop(0, seq // Q_TILE, q_body, 0)


def _attention(q, k, v):
    b, s, _ = q.shape
    nb = s // KEY_BLOCK
    spec = pl.BlockSpec((None, s, LANES), lambda bi, j: (bi, 0, j))
    return pl.pallas_call(
        functools.partial(_attn_kernel, seq=s),
        grid=(b, SB_WIDTH // LANES),
        in_specs=[spec, spec, spec],
        out_specs=spec,
        out_shape=jax.ShapeDtypeStruct((b, s, SB_WIDTH), BF16),
        scratch_shapes=[
            pltpu.VMEM((nb, LANES, 2 * KEY_BLOCK), BF16),
            pltpu.VMEM((nb, 2 * KEY_BLOCK, LANES), BF16),
            pltpu.VMEM((4 * KEY_BLOCK, 2 * KEY_BLOCK), BF16),
            pltpu.VMEM((Q_TILE, LANES), F32),
            pltpu.VMEM((Q_TILE, LANES), F32),
        ],
        compiler_params=_params(2),
        name="stickbreak_attn",
    )(q, k, v)


def _mix0_kernel(a_ref, u_ref, uh_ref, x_ref, pw_ref, ps_ref, wo_ref, g_ref, b_ref, o_ref,
                 ubuf, s2, s4, s8):
    i = pl.program_id(1)
    ts = u_ref.shape[0]
    n = HALO + ts
    gd = POOL_GROUP_DIM
    ubuf[0:HALO, :] = jnp.where(i > 0, uh_ref[...], 0.0)
    ubuf[HALO:, :] = u_ref[...]
    s2[8:n, :] = ubuf[8:n, :] + ubuf[7:n - 1, :]
    s4[16:n, gd:] = s2[16:n, gd:] + s2[14:n - 2, gd:]
    s8[24:n, 2 * gd:] = s4[24:n, 2 * gd:] + s4[20:n - 4, 2 * gd:]
    s16 = s8[HALO:n, 3 * gd:] + s8[HALO - 8:n - 8, 3 * gd:]
    sums = (s2[HALO:n, 0:gd], s4[HALO:n, gd:2 * gd], s8[HALO:n, 2 * gd:3 * gd], s16)

    pos1 = (i * ts + 1 + lax.broadcasted_iota(jnp.int32, (ts, gd), 0)).astype(F32)
    mixed = []
    for g, w in enumerate(POOL_WINDOWS):
        cols = slice(g * gd, (g + 1) * gd)
        pooled = sums[g] / jnp.minimum(pos1, float(w)) - u_ref[:, cols]
        mixed.append((_dot(pooled.astype(BF16), pw_ref[g]) * ps_ref[:, cols]).astype(BF16))
    cat = jnp.concatenate([a_ref[...]] + mixed, axis=1)
    mix = _dot(cat, wo_ref[...])
    o_ref[...] = _layer_norm(ALPHA * x_ref[...] + mix, g_ref[...], b_ref[...])


def _mix0(a, u, x, pool_w, pool_scale, w_out, ln_g, ln_b):
    b, s, _ = x.shape
    ts = ROW_TILE
    tile = lambda bi, i: (bi, i, 0)
    halo = lambda bi, i: (bi, jnp.maximum(i * (ts // HALO) - 1, 0), 0)
    return pl.pallas_call(
        _mix0_kernel,
        grid=(b, s // ts),
        in_specs=[
            pl.BlockSpec((None, ts, SB_WIDTH), tile),
            pl.BlockSpec((None, ts, POOL_WIDTH), tile),
            pl.BlockSpec((None, HALO, POOL_WIDTH), halo),
            pl.BlockSpec((None, ts, D_MODEL), tile),
            _const_spec(pool_w.shape), _const_spec(pool_scale.shape), _const_spec(w_out.shape),
            _const_spec(ln_g.shape), _const_spec(ln_b.shape),
        ],
        out_specs=pl.BlockSpec((None, ts, D_MODEL), tile),
        out_shape=jax.ShapeDtypeStruct((b, s, D_MODEL), F32),
        scratch_shapes=[pltpu.VMEM((HALO + ts, POOL_WIDTH), F32)] * 4,
        compiler_params=_params(2),
        name="mix0",
    )(a, u, u, x, pool_w, pool_scale, w_out, ln_g, ln_b)


def _ffn_kernel(x_ref, p_ref, wg_ref, wu_ref, wd_ref, g_ref, b_ref, wpg_ref, bpg_ref, wpp_ref, o_ref,
                h_ref):
    x = x_ref[...]
    xb = x.astype(BF16)
    for c in range(D_FF // FF_CHUNK):
        cols = slice(c * FF_CHUNK, (c + 1) * FF_CHUNK)
        gate = _dot(xb, wg_ref[:, cols])
        up = _dot(xb, wu_ref[:, cols])
        h_ref[:, cols] = (jax.nn.silu(gate) * up).astype(BF16)
    y = _dot(h_ref[...], wd_ref[...])
    x2 = _layer_norm(ALPHA * x + y, g_ref[...], b_ref[...])
    ple_gate = jax.nn.sigmoid(_dot(x2.astype(BF16), wpg_ref[...]) + bpg_ref[...])
    o_ref[...] = x2 + ple_gate * _dot(p_ref[...].astype(BF16), wpp_ref[...])


def _ffn(x2d, p3d, layer, w_gate, w_up, w_down, ln_g, ln_b, w_pg, b_pg, w_pp):
    t = x2d.shape[0]
    row = lambda i: (i, 0)
    consts = (w_gate, w_up, w_down, ln_g, ln_b, w_pg, b_pg, w_pp)
    return pl.pallas_call(
        _ffn_kernel,
        grid=(t // ROW_TILE,),
        in_specs=[pl.BlockSpec((ROW_TILE, D_MODEL), row),
                  pl.BlockSpec((None, ROW_TILE, PLE_DIM), lambda i: (layer, i, 0))]
        + [_const_spec(c.shape) for c in consts],
        out_specs=pl.BlockSpec((ROW_TILE, D_MODEL), row),
        out_shape=jax.ShapeDtypeStruct((t, D_MODEL), F32),
        scratch_shapes=[pltpu.VMEM((ROW_TILE, D_FF), BF16)],
        compiler_params=_params(1),
        name="ffn_ple",
    )(x2d, p3d, *consts)


def _inproj1_kernel(x_ref, w_ref, g_ref, b_ref, hc_ref, u_ref, v_ref):
    xb = x_ref[...].astype(BF16)
    c = CONV_WIDTH
    a = _dot(xb, w_ref[:, 0:c])
    gl = _dot(xb, w_ref[:, c:2 * c])
    hc_ref[...] = a * jax.nn.sigmoid(gl)
    u_ref[...] = jax.nn.gelu(_dot(xb, w_ref[:, 2 * c:2 * c + SG_WIDTH]))
    zv = jax.nn.gelu(_dot(xb, w_ref[:, 2 * c + SG_WIDTH:]))
    v_ref[...] = _layer_norm(zv, g_ref[...], b_ref[...]).astype(BF16)


def _inproj1(x2d, w_in, sg_ln_g, sg_ln_b):
    t = x2d.shape[0]
    row = lambda i: (i, 0)
    return pl.pallas_call(
        _inproj1_kernel,
        grid=(t // ROW_TILE,),
        in_specs=[pl.BlockSpec((ROW_TILE, D_MODEL), row), _const_spec(w_in.shape),
                  _const_spec(sg_ln_g.shape), _const_spec(sg_ln_b.shape)],
        out_specs=[pl.BlockSpec((ROW_TILE, CONV_WIDTH), row)] * 3,
        out_shape=[jax.ShapeDtypeStruct((t, CONV_WIDTH), F32), jax.ShapeDtypeStruct((t, SG_WIDTH), F32),
                   jax.ShapeDtypeStruct((t, SG_WIDTH), BF16)],
        compiler_params=_params(1),
        name="inproj1",
    )(x2d, w_in, sg_ln_g, sg_ln_b)


def _mix1_kernel(hc_ref, hh_ref, u_ref, v_ref, x_ref, dw_ref, cg_ref, cb_ref, sw_ref, sb_ref, wo_ref,
                 g_ref, b_ref, o_ref, shift_ref, conv_ref, cat_ref):
    i = pl.program_id(1)
    ts = hc_ref.shape[0]
    n = HALO + ts
    shift_ref[0, 0:HALO, :] = jnp.where(i > 0, hh_ref[...], 0.0)
    shift_ref[0, HALO:, :] = hc_ref[...]
    for r in range(1, SUBLANES):
        shift_ref[r, 0:n - SUBLANES, :] = shift_ref[0, r:r + n - SUBLANES, :]

    first = HALO - (CONV_TAPS - 1)
    for lb in range(CONV_WIDTH // LANES):
        cols = slice(lb * LANES, (lb + 1) * LANES)
        for rb in range(ts // CONV_ROWS):
            acc = jnp.zeros((CONV_ROWS, LANES), F32)
            for kk in range(CONV_TAPS):
                r0 = rb * CONV_ROWS + (first + kk) // SUBLANES * SUBLANES
                acc = acc + dw_ref[kk:kk + 1, cols] * shift_ref[(first + kk) % SUBLANES, r0:r0 + CONV_ROWS, cols]
            conv_ref[rb * CONV_ROWS:(rb + 1) * CONV_ROWS, cols] = acc
    cat_ref[:, 0:CONV_WIDTH] = jax.nn.silu(_layer_norm(conv_ref[...], cg_ref[...], cb_ref[...])).astype(BF16)

    tt = lax.broadcasted_iota(jnp.int32, (SG_CHUNK, SG_CHUNK), 0)
    ss = lax.broadcasted_iota(jnp.int32, (SG_CHUNK, SG_CHUNK), 1)
    n_chunks = ts // SG_CHUNK
    for g in range(SG_GROUPS):
        cols = slice(g * SG_GROUP_DIM, (g + 1) * SG_GROUP_DIM)
        wm = jnp.where(ss <= tt, sw_ref[g], 0.0).astype(BF16)
        vcat = jnp.concatenate([v_ref[c * SG_CHUNK:(c + 1) * SG_CHUNK, cols] for c in range(n_chunks)], axis=1)
        sv = _dot(wm, vcat)
        for c in range(n_chunks):
            rows = slice(c * SG_CHUNK, (c + 1) * SG_CHUNK)
            gated = u_ref[rows, cols] * (sv[:, c * SG_GROUP_DIM:(c + 1) * SG_GROUP_DIM] + sb_ref[:, cols])
            cat_ref[rows, CONV_WIDTH + g * SG_GROUP_DIM:CONV_WIDTH + (g + 1) * SG_GROUP_DIM] = gated.astype(BF16)

    mix = _dot(cat_ref[...], wo_ref[...])
    o_ref[...] = _layer_norm(ALPHA * x_ref[...] + mix, g_ref[...], b_ref[...])


def _mix1(hc, u, v, x, conv_dw, conv_g, conv_b, sg_w, sg_bias, w_out, ln_g, ln_b):
    b, s, _ = x.shape
    ts = ROW_TILE
    tile = lambda bi, i: (bi, i, 0)
    halo = lambda bi, i: (bi, jnp.maximum(i * (ts // HALO) - 1, 0), 0)
    consts = (conv_dw, conv_g, conv_b, sg_w, sg_bias, w_out, ln_g, ln_b)
    return pl.pallas_call(
        _mix1_kernel,
        grid=(b, s // ts),
        in_specs=[
            pl.BlockSpec((None, ts, CONV_WIDTH), tile),
            pl.BlockSpec((None, HALO, CONV_WIDTH), halo),
            pl.BlockSpec((None, ts, SG_WIDTH), tile),
            pl.BlockSpec((None, ts, SG_WIDTH), tile),
            pl.BlockSpec((None, ts, D_MODEL), tile),
        ] + [_const_spec(c.shape) for c in consts],
        out_specs=pl.BlockSpec((None, ts, D_MODEL), tile),
        out_shape=jax.ShapeDtypeStruct((b, s, D_MODEL), F32),
        scratch_shapes=[pltpu.VMEM((SUBLANES, HALO + ts, CONV_WIDTH), F32), pltpu.VMEM((ts, CONV_WIDTH), F32),
                        pltpu.VMEM((ts, MIX_WIDTH), BF16)],
        compiler_params=_params(2),
        name="mix1",
    )(hc, hc, u, v, x, *consts)


def _row(vec):
    return vec.reshape(1, -1)


def kernel(x, p, even_w_in, even_w_out, pool_w, pool_scale, odd_w_in, odd_w_out, conv_dw, conv_ln_g,
           conv_ln_b, sg_ln_g, sg_ln_b, sg_w, sg_b, ln_mix_g, ln_mix_b, ffn_w_gate, ffn_w_up, ffn_w_down,
           ln_ffn_g, ln_ffn_b, ple_w_proj, ple_w_gate, ple_b_gate):
    b, s, d = x.shape
    t = b * s
    bf = lambda w: w.astype(BF16)

    p3d = p.reshape(DEPTH, t, PLE_DIM)

    def ffn(xin, i):
        return _ffn(xin.reshape(t, d), p3d, i, bf(ffn_w_gate[i]), bf(ffn_w_up[i]),
                    bf(ffn_w_down[i]), _row(ln_ffn_g[i]), _row(ln_ffn_b[i]), bf(ple_w_gate[i]),
                    _row(ple_b_gate[i]), bf(ple_w_proj[i])).reshape(b, s, d)

    q, k, v, u = _inproj0(x.reshape(t, d), bf(even_w_in[0]))
    attn = _attention(q.reshape(b, s, SB_WIDTH), k.reshape(b, s, SB_WIDTH), v.reshape(b, s, SB_WIDTH))
    x = _mix0(attn, u.reshape(b, s, POOL_WIDTH), x, bf(pool_w[0]), _row(pool_scale[0]), bf(even_w_out[0]),
              _row(ln_mix_g[0]), _row(ln_mix_b[0]))
    x = ffn(x, 0)

    hc, su, sv = _inproj1(x.reshape(t, d), bf(odd_w_in[0]), _row(sg_ln_g[0]), _row(sg_ln_b[0]))
    sg_bias = jnp.repeat(sg_b[0].T, SG_GROUP_DIM, axis=1)
    x = _mix1(hc.reshape(b, s, CONV_WIDTH), su.reshape(b, s, SG_WIDTH), sv.reshape(b, s, SG_WIDTH), x,
              conv_dw[0], _row(conv_ln_g[0]), _row(conv_ln_b[0]), sg_w[0], sg_bias, bf(odd_w_out[0]),
              _row(ln_mix_g[1]), _row(ln_mix_b[1]))
    return ffn(x, 1)
```

```python
import functools

import jax
import jax.numpy as jnp
from jax import lax
from jax.experimental import pallas as pl
from jax.experimental.pallas import tpu as pltpu

F32 = jnp.float32
BF16 = jnp.bfloat16

D_MODEL = 1024
N_SB_HEADS = 8
SB_HEAD_DIM = 64
SB_WIDTH = N_SB_HEADS * SB_HEAD_DIM
POOL_WINDOWS = (2, 4, 8, 16)
POOL_GROUP_DIM = 128
POOL_WIDTH = len(POOL_WINDOWS) * POOL_GROUP_DIM
CONV_WIDTH = 512
CONV_TAPS = 31
SG_GROUPS = 4
SG_GROUP_DIM = 128
SG_WIDTH = SG_GROUPS * SG_GROUP_DIM
SG_CHUNK = 128
MIX_WIDTH = 1024
D_FF = 2816
PLE_DIM = 256
DEPTH = 2
ALPHA = (2 * DEPTH) ** 0.25
LN_EPS = 1e-5

LANES = 128
SUBLANES = 8
MXU_DIM = 256
VMEM_LIMIT = 56 * 1024 * 1024

ROW_TILE = 1024
CONV_TILE = 512
SUB_ROWS = 256
HALO = 32
KEY_BLOCK = 128
Q_TILE = 256
EXP_UNDERFLOW_LOG = -104.0
FF_CHUNK = 256
CONV_ROWS = 64


def _layer_norm(y, g, b):
    mu = jnp.mean(y, axis=-1, keepdims=True)
    d = y - mu
    var = jnp.mean(d * d, axis=-1, keepdims=True)
    return d * lax.rsqrt(var + LN_EPS) * g + b


def _dot(a, b):
    return jnp.dot(a, b, preferred_element_type=F32)


def _const_spec(shape):
    zeros = (0,) * len(shape)
    return pl.BlockSpec(shape, lambda *_: zeros, pipeline_mode=pl.Buffered(1))


def _params(n_axes):
    return pltpu.CompilerParams(dimension_semantics=("arbitrary",) * n_axes,
                                vmem_limit_bytes=VMEM_LIMIT)


def _skewed(n_sub, stages):
    values = [None] * n_sub
    for step in range(n_sub + len(stages) - 1):
        for k, stage in enumerate(stages):
            s = step - k
            if 0 <= s < n_sub:
                values[s] = stage(s, values[s])


def _inproj0_kernel(x_ref, w_ref, q_ref, k_ref, v_ref, u_ref):
    w = SB_WIDTH
    for s in range(x_ref.shape[0] // SUB_ROWS):
        rows = slice(s * SUB_ROWS, (s + 1) * SUB_ROWS)
        xb = x_ref[rows, :].astype(BF16)
        q_ref[rows, :] = (_dot(xb, w_ref[:, 0:w]) * (SB_HEAD_DIM ** -0.5)).astype(BF16)
        k_ref[rows, :] = _dot(xb, w_ref[:, w:2 * w]).astype(BF16)
        v_ref[rows, :] = _dot(xb, w_ref[:, 2 * w:3 * w]).astype(BF16)
        u_ref[rows, :] = _dot(xb, w_ref[:, 3 * w:])


def _inproj0(x2d, w_in):
    t = x2d.shape[0]
    row = lambda i: (i, 0)
    return pl.pallas_call(
        _inproj0_kernel,
        grid=(t // ROW_TILE,),
        in_specs=[pl.BlockSpec((ROW_TILE, D_MODEL), row), _const_spec(w_in.shape)],
        out_specs=[pl.BlockSpec((ROW_TILE, SB_WIDTH), row)] * 3 + [pl.BlockSpec((ROW_TILE, POOL_WIDTH), row)],
        out_shape=[jax.ShapeDtypeStruct((t, SB_WIDTH), BF16)] * 3 + [jax.ShapeDtypeStruct((t, POOL_WIDTH), F32)],
        compiler_params=_params(1),
        name="inproj0",
    )(x2d, w_in)


def _attn_kernel(q_ref, k_ref, v_ref, o_ref, kt_ref, vcat_ref, sfx_ref, acc_ref, carry_ref, *, seq):
    nb = seq // KEY_BLOCK
    kb_cols = 2 * KEY_BLOCK
    half = LANES // 2
    per_tile = Q_TILE // KEY_BLOCK

    kt = k_ref[...].astype(F32).T
    dim = lax.broadcasted_iota(jnp.int32, kt.shape, 0)
    kt_a = jnp.where(dim < half, kt, 0.0).astype(BF16)
    kt_b = jnp.where(dim < half, 0.0, kt).astype(BF16)
    for kb in range(nb):
        kt_ref[kb, :, 0:KEY_BLOCK] = kt_a[:, kb * KEY_BLOCK:(kb + 1) * KEY_BLOCK]
        kt_ref[kb, :, KEY_BLOCK:] = kt_b[:, kb * KEY_BLOCK:(kb + 1) * KEY_BLOCK]
    lane = lax.broadcasted_iota(jnp.int32, (seq, LANES), 1)
    v = v_ref[...]
    zero = jnp.zeros_like(v)
    vcat_ref[:, 0:KEY_BLOCK, :] = jnp.where(lane < half, v, zero).reshape(nb, KEY_BLOCK, LANES)
    vcat_ref[:, KEY_BLOCK:, :] = jnp.where(lane < half, zero, v).reshape(nb, KEY_BLOCK, LANES)

    rj = lax.broadcasted_iota(jnp.int32, (2 * kb_cols, kb_cols), 0)
    cs = lax.broadcasted_iota(jnp.int32, (2 * kb_cols, kb_cols), 1)
    rj = jnp.where(rj >= kb_cols, rj - kb_cols, rj)
    in_suffix = ((rj < KEY_BLOCK) == (cs < KEY_BLOCK)) & (rj >= cs)
    sfx_ref[...] = jnp.where(in_suffix, -1.0, 0.0).astype(BF16)
    out_is_a = lax.broadcasted_iota(jnp.int32, (Q_TILE, LANES), 1) < half

    row = lax.broadcasted_iota(jnp.int32, (Q_TILE, kb_cols), 0)
    col = lax.broadcasted_iota(jnp.int32, (Q_TILE, kb_cols), 1)
    col = jnp.where(col >= KEY_BLOCK, col - KEY_BLOCK, col)

    def visit(qb, blocks):
        zs = [_dot(qb, kt_ref[kb]) for kb, _ in blocks]
        sps, masks = [], []
        for z, (_, key_offset) in zip(zs, blocks):
            sp = jnp.maximum(z, 0.0) + jnp.log(1.0 + jnp.exp(-jnp.abs(z)))
            causal = None if key_offset is None else col + key_offset < row
            masks.append(causal)
            sps.append(sp if causal is None else jnp.where(causal, sp, 0.0))
        his = [sp.astype(BF16) for sp in sps]
        los = [(sp - hi.astype(F32)).astype(BF16) for sp, hi in zip(sps, his)]
        sums = [_dot(jnp.concatenate([hi, lo], axis=1), sfx_ref[...]) for hi, lo in zip(his, los)]
        pvs = []
        for s, z, causal, (kb, _) in zip(sums, zs, masks, blocks):
            w = jnp.exp(z + s)
            if causal is not None:
                w = jnp.where(causal, w, 0.0)
            pvs.append(_dot(w.astype(BF16), vcat_ref[kb]))
        acc = acc_ref[...]
        carry = carry_ref[...]
        for pv, sp in zip(pvs, sps):
            acc = acc + jnp.exp(carry) * pv
            tot_a = jnp.sum(sp[:, 0:KEY_BLOCK], axis=1, keepdims=True)
            tot_b = jnp.sum(sp[:, KEY_BLOCK:], axis=1, keepdims=True)
            carry = carry - jnp.where(out_is_a, tot_a, tot_b)
        acc_ref[...] = acc
        carry_ref[...] = carry

    def q_body(i, _):
        q0 = pl.multiple_of(i * Q_TILE, Q_TILE)
        qb = q_ref[pl.ds(q0, Q_TILE), :]
        acc_ref[...] = jnp.zeros_like(acc_ref)
        carry_ref[...] = jnp.zeros_like(carry_ref)
        diagonal = [(i * per_tile + d, d * KEY_BLOCK) for d in reversed(range(per_tile))]

        @pl.when(i == 0)
        def _():
            visit(qb, diagonal)

        @pl.when(i > 0)
        def _():
            visit(qb, diagonal + [(i * per_tile - 1 - d, None) for d in range(per_tile)])

        def max_carry():
            return jnp.max(carry_ref[...])

        left = jnp.maximum(i - 1, 0) * per_tile
        trips = left // (2 * per_tile)

        def k_cond(state):
            n, top = state
            return (n < trips) & (top > EXP_UNDERFLOW_LOG)

        def k_body(state):
            n, _ = state
            first = left - 1 - n * 2 * per_tile
            visit(qb, [(first - d, None) for d in range(2 * per_tile)])
            return n + 1, max_carry()

        n_done, top = lax.while_loop(k_cond, k_body, (0, max_carry()))

        @pl.when((n_done == trips) & (left - trips * 2 * per_tile > 0) & (top > EXP_UNDERFLOW_LOG))
        def _():
            visit(qb, [(per_tile - 1 - d, None) for d in range(per_tile)])

        o_ref[pl.ds(q0, Q_TILE), :] = acc_ref[...].astype(o_ref.dtype)
        return 0

    lax.fori_loop(0, seq // Q_TILE, q_body, 0)


def _attention(q, k, v):
    b, s, _ = q.shape
    nb = s // KEY_BLOCK
    spec = pl.BlockSpec((None, s, LANES), lambda bi, j: (bi, 0, j))
    return pl.pallas_call(
        functools.partial(_attn_kernel, seq=s),
        grid=(b, SB_WIDTH // LANES),
        in_specs=[spec, spec, spec],
        out_specs=spec,
        out_shape=jax.ShapeDtypeStruct((b, s, SB_WIDTH), BF16),
        scratch_shapes=[
            pltpu.VMEM((nb, LANES, 2 * KEY_BLOCK), BF16),
            pltpu.VMEM((nb, 2 * KEY_BLOCK, LANES), BF16),
            pltpu.VMEM((4 * KEY_BLOCK, 2 * KEY_BLOCK), BF16),
            pltpu.VMEM((Q_TILE, LANES), F32),
            pltpu.VMEM((Q_TILE, LANES), F32),
        ],
        compiler_params=_params(2),
        name="stickbreak_attn",
    )(q, k, v)


def _mix0_kernel(a_ref, u_ref, uh_ref, x_ref, pw_ref, ps_ref, wo_ref, g_ref, b_ref, o_ref,
                 ubuf, s2, s4, s8, s16_ref):
    i = pl.program_id(1)
    ts = u_ref.shape[0]
    n = HALO + ts
    gd = POOL_GROUP_DIM
    ubuf[0:HALO, :] = jnp.where(i > 0, uh_ref[...], 0.0)
    ubuf[HALO:, :] = u_ref[...]
    s2[8:n, :] = ubuf[8:n, :] + ubuf[7:n - 1, :]
    s4[16:n, gd:] = s2[16:n, gd:] + s2[14:n - 2, gd:]
    s8[24:n, 2 * gd:] = s4[24:n, 2 * gd:] + s4[20:n - 4, 2 * gd:]
    s16_ref[HALO:n, 3 * gd:] = s8[HALO:n, 3 * gd:] + s8[HALO - 8:n - 8, 3 * gd:]
    window_sums = (s2, s4, s8, s16_ref)

    def project(s, _):
        rows = slice(s * SUB_ROWS, (s + 1) * SUB_ROWS)
        held = slice(HALO + s * SUB_ROWS, HALO + (s + 1) * SUB_ROWS)
        pos1 = (i * ts + s * SUB_ROWS + 1
                + lax.broadcasted_iota(jnp.int32, (SUB_ROWS, gd), 0)).astype(F32)
        mixed = []
        for g, w in enumerate(POOL_WINDOWS):
            cols = slice(g * gd, (g + 1) * gd)
            pooled = window_sums[g][held, cols] / jnp.minimum(pos1, float(w)) - u_ref[rows, cols]
            mixed.append((_dot(pooled.astype(BF16), pw_ref[g]) * ps_ref[:, cols]).astype(BF16))
        cat = jnp.concatenate([a_ref[rows, :]] + mixed, axis=1)
        return _dot(cat, wo_ref[...])

    def normalize(s, mix):
        rows = slice(s * SUB_ROWS, (s + 1) * SUB_ROWS)
        o_ref[rows, :] = _layer_norm(ALPHA * x_ref[rows, :] + mix, g_ref[...], b_ref[...])

    _skewed(ts // SUB_ROWS, [project, normalize])


def _mix0(a, u, x, pool_w, pool_scale, w_out, ln_g, ln_b):
    b, s, _ = x.shape
    ts = ROW_TILE
    tile = lambda bi, i: (bi, i, 0)
    halo = lambda bi, i: (bi, jnp.maximum(i * (ts // HALO) - 1, 0), 0)
    return pl.pallas_call(
        _mix0_kernel,
        grid=(b, s // ts),
        in_specs=[
            pl.BlockSpec((None, ts, SB_WIDTH), tile),
            pl.BlockSpec((None, ts, POOL_WIDTH), tile),
            pl.BlockSpec((None, HALO, POOL_WIDTH), halo),
            pl.BlockSpec((None, ts, D_MODEL), tile),
            _const_spec(pool_w.shape), _const_spec(pool_scale.shape), _const_spec(w_out.shape),
            _const_spec(ln_g.shape), _const_spec(ln_b.shape),
        ],
        out_specs=pl.BlockSpec((None, ts, D_MODEL), tile),
        out_shape=jax.ShapeDtypeStruct((b, s, D_MODEL), F32),
        scratch_shapes=[pltpu.VMEM((HALO + ts, POOL_WIDTH), F32)] * 5,
        compiler_params=_params(2),
        name="mix0",
    )(a, u, u, x, pool_w, pool_scale, w_out, ln_g, ln_b)


def _ffn_kernel(x_ref, p_ref, wg_ref, wu_ref, wd_ref, g_ref, b_ref, wpg_ref, bpg_ref, wpp_ref, o_ref,
                h_ref):
    def rows(s):
        return slice(s * SUB_ROWS, (s + 1) * SUB_ROWS)

    def swiglu(s, _):
        xb = x_ref[rows(s), :].astype(BF16)
        for c in range(D_FF // FF_CHUNK):
            cols = slice(c * FF_CHUNK, (c + 1) * FF_CHUNK)
            gate = _dot(xb, wg_ref[:, cols])
            up = _dot(xb, wu_ref[:, cols])
            h_ref[rows(s), cols] = (jax.nn.silu(gate) * up).astype(BF16)
        return _dot(h_ref[rows(s), :], wd_ref[...])

    def norm_and_embed(s, y):
        x2 = _layer_norm(ALPHA * x_ref[rows(s), :] + y, g_ref[...], b_ref[...])
        ple_gate = jax.nn.sigmoid(_dot(x2.astype(BF16), wpg_ref[...]) + bpg_ref[...])
        o_ref[rows(s), :] = x2 + ple_gate * _dot(p_ref[rows(s), :].astype(BF16), wpp_ref[...])

    _skewed(x_ref.shape[0] // SUB_ROWS, [swiglu, norm_and_embed])


def _ffn(x2d, p3d, layer, w_gate, w_up, w_down, ln_g, ln_b, w_pg, b_pg, w_pp):
    t = x2d.shape[0]
    row = lambda i: (i, 0)
    consts = (w_gate, w_up, w_down, ln_g, ln_b, w_pg, b_pg, w_pp)
    return pl.pallas_call(
        _ffn_kernel,
        grid=(t // ROW_TILE,),
        in_specs=[pl.BlockSpec((ROW_TILE, D_MODEL), row),
                  pl.BlockSpec((None, ROW_TILE, PLE_DIM), lambda i: (layer, i, 0))]
        + [_const_spec(c.shape) for c in consts],
        out_specs=pl.BlockSpec((ROW_TILE, D_MODEL), row),
        out_shape=jax.ShapeDtypeStruct((t, D_MODEL), F32),
        scratch_shapes=[pltpu.VMEM((ROW_TILE, D_FF), BF16)],
        compiler_params=_params(1),
        name="ffn_ple",
    )(x2d, p3d, *consts)


def _inproj1_kernel(x_ref, w_ref, g_ref, b_ref, hc_ref, u_ref, v_ref):
    c = CONV_WIDTH

    def rows(s):
        return slice(s * SUB_ROWS, (s + 1) * SUB_ROWS)

    def project(s, _):
        xb = x_ref[rows(s), :].astype(BF16)
        return [_dot(xb, w_ref[:, lo:lo + c]) for lo in (0, c, 2 * c, 2 * c + SG_WIDTH)]

    def activate(s, h):
        a, gl, zu, zv = h
        hc_ref[rows(s), :] = a * jax.nn.sigmoid(gl)
        u_ref[rows(s), :] = jax.nn.gelu(zu)
        v_ref[rows(s), :] = _layer_norm(jax.nn.gelu(zv), g_ref[...], b_ref[...]).astype(BF16)

    _skewed(x_ref.shape[0] // SUB_ROWS, [project, activate])


def _inproj1(x2d, w_in, sg_ln_g, sg_ln_b):
    t = x2d.shape[0]
    row = lambda i: (i, 0)
    return pl.pallas_call(
        _inproj1_kernel,
        grid=(t // ROW_TILE,),
        in_specs=[pl.BlockSpec((ROW_TILE, D_MODEL), row), _const_spec(w_in.shape),
                  _const_spec(sg_ln_g.shape), _const_spec(sg_ln_b.shape)],
        out_specs=[pl.BlockSpec((ROW_TILE, CONV_WIDTH), row)] * 3,
        out_shape=[jax.ShapeDtypeStruct((t, CONV_WIDTH), F32), jax.ShapeDtypeStruct((t, SG_WIDTH), F32),
                   jax.ShapeDtypeStruct((t, SG_WIDTH), BF16)],
        compiler_params=_params(1),
        name="inproj1",
    )(x2d, w_in, sg_ln_g, sg_ln_b)


def _mix1_kernel(hc_ref, hh_ref, u_ref, v_ref, x_ref, dw_ref, cg_ref, cb_ref, sw_ref, sb_ref, wo_ref,
                 g_ref, b_ref, o_ref, shift_ref, conv_ref, cat_ref):
    i = pl.program_id(1)
    ts = hc_ref.shape[0]
    n = HALO + ts
    shift_ref[0, 0:HALO, :] = jnp.where(i > 0, hh_ref[...], 0.0)
    shift_ref[0, HALO:, :] = hc_ref[...]
    for r in range(1, SUBLANES):
        shift_ref[r, 0:n - SUBLANES, :] = shift_ref[0, r:r + n - SUBLANES, :]

    tt = lax.broadcasted_iota(jnp.int32, (SG_CHUNK, SG_CHUNK), 0)
    ss = lax.broadcasted_iota(jnp.int32, (SG_CHUNK, SG_CHUNK), 1)
    gate_w = [jnp.where(ss <= tt, sw_ref[g], 0.0).astype(BF16) for g in range(SG_GROUPS)]
    first = HALO - (CONV_TAPS - 1)

    def mixers(s, _):
        base = s * SUB_ROWS
        rows = slice(base, base + SUB_ROWS)
        for lb in range(CONV_WIDTH // LANES):
            cols = slice(lb * LANES, (lb + 1) * LANES)
            for rb in range(SUB_ROWS // CONV_ROWS):
                acc = jnp.zeros((CONV_ROWS, LANES), F32)
                for kk in range(CONV_TAPS):
                    r0 = base + rb * CONV_ROWS + (first + kk) // SUBLANES * SUBLANES
                    acc = acc + (dw_ref[kk:kk + 1, cols]
                                 * shift_ref[(first + kk) % SUBLANES, r0:r0 + CONV_ROWS, cols])
                conv_ref[base + rb * CONV_ROWS:base + (rb + 1) * CONV_ROWS, cols] = acc
        cat_ref[rows, 0:CONV_WIDTH] = jax.nn.silu(
            _layer_norm(conv_ref[rows, :], cg_ref[...], cb_ref[...])).astype(BF16)
        chunks = [slice(base + c * SG_CHUNK, base + (c + 1) * SG_CHUNK) for c in range(SUB_ROWS // SG_CHUNK)]
        for g in range(SG_GROUPS):
            cols = slice(g * SG_GROUP_DIM, (g + 1) * SG_GROUP_DIM)
            sv = _dot(gate_w[g], jnp.concatenate([v_ref[ch, cols] for ch in chunks], axis=1))
            for c, ch in enumerate(chunks):
                gated = u_ref[ch, cols] * (sv[:, c * SG_GROUP_DIM:(c + 1) * SG_GROUP_DIM] + sb_ref[:, cols])
                cat_ref[ch, CONV_WIDTH + g * SG_GROUP_DIM:CONV_WIDTH + (g + 1) * SG_GROUP_DIM] = gated.astype(BF16)

    def project(s, _):
        return _dot(cat_ref[s * SUB_ROWS:(s + 1) * SUB_ROWS, :], wo_ref[...])

    def normalize(s, mix):
        rows = slice(s * SUB_ROWS, (s + 1) * SUB_ROWS)
        o_ref[rows, :] = _layer_norm(ALPHA * x_ref[rows, :] + mix, g_ref[...], b_ref[...])

    _skewed(ts // SUB_ROWS, [mixers, project, normalize])


def _mix1(hc, u, v, x, conv_dw, conv_g, conv_b, sg_w, sg_bias, w_out, ln_g, ln_b):
    b, s, _ = x.shape
    ts = CONV_TILE
    tile = lambda bi, i: (bi, i, 0)
    halo = lambda bi, i: (bi, jnp.maximum(i * (ts // HALO) - 1, 0), 0)
    consts = (conv_dw, conv_g, conv_b, sg_w, sg_bias, w_out, ln_g, ln_b)
    return pl.pallas_call(
        _mix1_kernel,
        grid=(b, s // ts),
        in_specs=[
            pl.BlockSpec((None, ts, CONV_WIDTH), tile),
            pl.BlockSpec((None, HALO, CONV_WIDTH), halo),
            pl.BlockSpec((None, ts, SG_WIDTH), tile),
            pl.BlockSpec((None, ts, SG_WIDTH), tile),
            pl.BlockSpec((None, ts, D_MODEL), tile),
        ] + [_const_spec(c.shape) for c in consts],
        out_specs=pl.BlockSpec((None, ts, D_MODEL), tile),
        out_shape=jax.ShapeDtypeStruct((b, s, D_MODEL), F32),
        scratch_shapes=[pltpu.VMEM((SUBLANES, HALO + ts, CONV_WIDTH), F32), pltpu.VMEM((ts, CONV_WIDTH), F32),
                        pltpu.VMEM((ts, MIX_WIDTH), BF16)],
        compiler_params=_params(2),
        name="mix1",
    )(hc, hc, u, v, x, *consts)


def _row(vec):
    return vec.reshape(1, -1)


def kernel(x, p, even_w_in, even_w_out, pool_w, pool_scale, odd_w_in, odd_w_out, conv_dw, conv_ln_g,
           conv_ln_b, sg_ln_g, sg_ln_b, sg_w, sg_b, ln_mix_g, ln_mix_b, ffn_w_gate, ffn_w_up, ffn_w_down,
           ln_ffn_g, ln_ffn_b, ple_w_proj, ple_w_gate, ple_b_gate):
    b, s, d = x.shape
    t = b * s
    bf = lambda w: w.astype(BF16)

    p3d = p.reshape(DEPTH, t, PLE_DIM)

    def ffn(xin, i):
        return _ffn(xin.reshape(t, d), p3d, i, bf(ffn_w_gate[i]), bf(ffn_w_up[i]),
                    bf(ffn_w_down[i]), _row(ln_ffn_g[i]), _row(ln_ffn_b[i]), bf(ple_w_gate[i]),
                    _row(ple_b_gate[i]), bf(ple_w_proj[i])).reshape(b, s, d)

    q, k, v, u = _inproj0(x.reshape(t, d), bf(even_w_in[0]))
    attn = _attention(q.reshape(b, s, SB_WIDTH), k.reshape(b, s, SB_WIDTH), v.reshape(b, s, SB_WIDTH))
    x = _mix0(attn, u.reshape(b, s, POOL_WIDTH), x, bf(pool_w[0]), _row(pool_scale[0]), bf(even_w_out[0]),
              _row(ln_mix_g[0]), _row(ln_mix_b[0]))
    x = ffn(x, 0)

    hc, su, sv = _inproj1(x.reshape(t, d), bf(odd_w_in[0]), _row(sg_ln_g[0]), _row(sg_ln_b[0]))
    sg_bias = jnp.repeat(sg_b[0].T, SG_GROUP_DIM, axis=1)
    x = _mix1(hc.reshape(b, s, CONV_WIDTH), su.reshape(b, s, SG_WIDTH), sv.reshape(b, s, SG_WIDTH), x,
              conv_dw[0], _row(conv_ln_g[0]), _row(conv_ln_b[0]), sg_w[0], sg_bias, bf(odd_w_out[0]),
              _row(ln_mix_g[1]), _row(ln_mix_b[1]))
    return ffn(x, 1)
```

```python
import functools

import jax
import jax.numpy as jnp
from jax import lax
from jax.experimental import pallas as pl
from jax.experimental.pallas import tpu as pltpu

F32 = jnp.float32
BF16 = jnp.bfloat16

D_MODEL = 1024
N_SB_HEADS = 8
SB_HEAD_DIM = 64
SB_WIDTH = N_SB_HEADS * SB_HEAD_DIM
POOL_WINDOWS = (2, 4, 8, 16)
POOL_GROUP_DIM = 128
POOL_WIDTH = len(POOL_WINDOWS) * POOL_GROUP_DIM
CONV_WIDTH = 512
CONV_TAPS = 31
SG_GROUPS = 4
SG_GROUP_DIM = 128
SG_WIDTH = SG_GROUPS * SG_GROUP_DIM
SG_CHUNK = 128
MIX_WIDTH = 1024
D_FF = 2816
PLE_DIM = 256
DEPTH = 2
ALPHA = (2 * DEPTH) ** 0.25
LN_EPS = 1e-5

LANES = 128
SUBLANES = 8
MXU_DIM = 256
VMEM_LIMIT = 56 * 1024 * 1024

ROW_TILE = 1024
CONV_TILE = 512
SUB_ROWS = 256
HALO = 32
KEY_BLOCK = 128
Q_TILE = 256
EXP_UNDERFLOW_LOG = -104.0
FF_CHUNK = 256
CONV_ROWS = 64


def _layer_norm(y, g, b):
    mu = jnp.mean(y, axis=-1, keepdims=True)
    d = y - mu
    var = jnp.mean(d * d, axis=-1, keepdims=True)
    return d * lax.rsqrt(var + LN_EPS) * g + b


def _dot(a, b):
    return jnp.dot(a, b, preferred_element_type=F32)


def _const_spec(shape):
    zeros = (0,) * len(shape)
    return pl.BlockSpec(shape, lambda *_: zeros, pipeline_mode=pl.Buffered(1))


def _params(n_axes):
    return pltpu.CompilerParams(dimension_semantics=("arbitrary",) * n_axes,
                                vmem_limit_bytes=VMEM_LIMIT)


def _skewed(n_sub, stages):
    values = [None] * n_sub
    for step in range(n_sub + len(stages) - 1):
        for k, stage in enumerate(stages):
            s = step - k
            if 0 <= s < n_sub:
                values[s] = stage(s, values[s])


def _inproj0_kernel(x_ref, w_ref, q_ref, k_ref, v_ref, u_ref):
    w = SB_WIDTH
    for s in range(x_ref.shape[0] // SUB_ROWS):
        rows = slice(s * SUB_ROWS, (s + 1) * SUB_ROWS)
        xb = x_ref[rows, :].astype(BF16)
        q_ref[rows, :] = (_dot(xb, w_ref[:, 0:w]) * (SB_HEAD_DIM ** -0.5)).astype(BF16)
        k_ref[rows, :] = _dot(xb, w_ref[:, w:2 * w]).astype(BF16)
        v_ref[rows, :] = _dot(xb, w_ref[:, 2 * w:3 * w]).astype(BF16)
        u_ref[rows, :] = _dot(xb, w_ref[:, 3 * w:])


def _inproj0(x2d, w_in):
    t = x2d.shape[0]
    row = lambda i: (i, 0)
    return pl.pallas_call(
        _inproj0_kernel,
        grid=(t // ROW_TILE,),
        in_specs=[pl.BlockSpec((ROW_TILE, D_MODEL), row), _const_spec(w_in.shape)],
        out_specs=[pl.BlockSpec((ROW_TILE, SB_WIDTH), row)] * 3 + [pl.BlockSpec((ROW_TILE, POOL_WIDTH), row)],
        out_shape=[jax.ShapeDtypeStruct((t, SB_WIDTH), BF16)] * 3 + [jax.ShapeDtypeStruct((t, POOL_WIDTH), F32)],
        compiler_params=_params(1),
        name="inproj0",
    )(x2d, w_in)


def _attn_kernel(q_ref, k_ref, v_ref, o_ref, kt_ref, vcat_ref, sfx_ref, acc_ref, carry_ref, *, seq):
    nb = seq // KEY_BLOCK
    kb_cols = 2 * KEY_BLOCK
    half = LANES // 2
    per_tile = Q_TILE // KEY_BLOCK

    kt = k_ref[...].astype(F32).T
    dim = lax.broadcasted_iota(jnp.int32, kt.shape, 0)
    kt_a = jnp.where(dim < half, kt, 0.0).astype(BF16)
    kt_b = jnp.where(dim < half, 0.0, kt).astype(BF16)
    for kb in range(nb):
        kt_ref[kb, :, 0:KEY_BLOCK] = kt_a[:, kb * KEY_BLOCK:(kb + 1) * KEY_BLOCK]
        kt_ref[kb, :, KEY_BLOCK:] = kt_b[:, kb * KEY_BLOCK:(kb + 1) * KEY_BLOCK]
    lane = lax.broadcasted_iota(jnp.int32, (seq, LANES), 1)
    v = v_ref[...]
    zero = jnp.zeros_like(v)
    vcat_ref[:, 0:KEY_BLOCK, :] = jnp.where(lane < half, v, zero).reshape(nb, KEY_BLOCK, LANES)
    vcat_ref[:, KEY_BLOCK:, :] = jnp.where(lane < half, zero, v).reshape(nb, KEY_BLOCK, LANES)

    rj = lax.broadcasted_iota(jnp.int32, (2 * kb_cols, kb_cols), 0)
    cs = lax.broadcasted_iota(jnp.int32, (2 * kb_cols, kb_cols), 1)
    rj = jnp.where(rj >= kb_cols, rj - kb_cols, rj)
    in_suffix = ((rj < KEY_BLOCK) == (cs < KEY_BLOCK)) & (rj >= cs)
    sfx_ref[...] = jnp.where(in_suffix, -1.0, 0.0).astype(BF16)
    out_is_a = lax.broadcasted_iota(jnp.int32, (KEY_BLOCK, LANES), 1) < half

    row = lax.broadcasted_iota(jnp.int32, (KEY_BLOCK, kb_cols), 0)
    col = lax.broadcasted_iota(jnp.int32, (KEY_BLOCK, kb_cols), 1)
    on_diagonal = jnp.where(col >= KEY_BLOCK, col - KEY_BLOCK, col) < row
    halves = (slice(0, KEY_BLOCK), slice(KEY_BLOCK, Q_TILE))

    def mask_diagonal(val):
        top = jnp.where(on_diagonal, val[0:KEY_BLOCK], 0.0)
        return top if val.shape[0] == KEY_BLOCK else jnp.concatenate([top, val[KEY_BLOCK:]], axis=0)

    def visit(jobs):
        flat = [(qb, slot, blk) for qb, slot, blocks in jobs for blk in blocks]
        zs = [_dot(qb[r0:r0 + nr], kt_ref[kb]) for qb, _, (kb, r0, nr, _) in flat]
        sps = []
        for z, (_, _, blk) in zip(zs, flat):
            sp = jnp.maximum(z, 0.0) + jnp.log(1.0 + jnp.exp(-jnp.abs(z)))
            sps.append(mask_diagonal(sp) if blk[3] else sp)
        his = [sp.astype(BF16) for sp in sps]
        los = [(sp - hi.astype(F32)).astype(BF16) for sp, hi in zip(sps, his)]
        sums = [_dot(jnp.concatenate([hi, lo], axis=1), sfx_ref[...]) for hi, lo in zip(his, los)]
        pvs = []
        for s, z, (_, _, (kb, _, _, diagonal)) in zip(sums, zs, flat):
            w = jnp.exp(z + s)
            if diagonal:
                w = mask_diagonal(w)
            pvs.append(_dot(w.astype(BF16), vcat_ref[kb]))
        for _, slot, _ in jobs:
            acc = [acc_ref[slot, h, :] for h in halves]
            carry = [carry_ref[slot, h, :] for h in halves]
            for pv, sp, (_, blk_slot, (_, r0, nr, _)) in zip(pvs, sps, flat):
                if blk_slot != slot:
                    continue
                tot_a = jnp.sum(sp[:, 0:KEY_BLOCK], axis=1, keepdims=True)
                tot_b = jnp.sum(sp[:, KEY_BLOCK:], axis=1, keepdims=True)
                for h, half in enumerate(halves):
                    if r0 <= half.start and half.stop <= r0 + nr:
                        mine = slice(half.start - r0, half.stop - r0)
                        acc[h] = acc[h] + jnp.exp(carry[h]) * pv[mine]
                        carry[h] = carry[h] - jnp.where(out_is_a, tot_a[mine], tot_b[mine])
            for h, half in enumerate(halves):
                acc_ref[slot, half, :] = acc[h]
                carry_ref[slot, half, :] = carry[h]

    def first_blocks(i, with_previous):
        kd = i * per_tile
        blocks = [(kd + 1, KEY_BLOCK, KEY_BLOCK, True), (kd, 0, Q_TILE, True)]
        if with_previous:
            blocks += [(kd - 1, 0, Q_TILE, False), (kd - 2, 0, KEY_BLOCK, False)]
        return blocks

    def walk_on(i, qb, slot):
        def max_carry():
            return jnp.max(carry_ref[slot])

        kd = i * per_tile

        @pl.when((i > 0) & (max_carry() > EXP_UNDERFLOW_LOG))
        def _():
            visit([(qb, slot, [(kd - 2, KEY_BLOCK, KEY_BLOCK, False)])])

        left = jnp.maximum(i - 1, 0) * per_tile
        trips = left // (2 * per_tile)

        def k_cond(state):
            n, top = state
            return (n < trips) & (top > EXP_UNDERFLOW_LOG)

        def k_body(state):
            n, _ = state
            first = left - 1 - n * 2 * per_tile
            visit([(qb, slot, [(first - d, 0, Q_TILE, False) for d in range(2 * per_tile)])])
            return n + 1, max_carry()

        n_done, top = lax.while_loop(k_cond, k_body, (0, max_carry()))

        @pl.when((n_done == trips) & (left - trips * 2 * per_tile > 0) & (top > EXP_UNDERFLOW_LOG))
        def _():
            visit([(qb, slot, [(per_tile - 1 - d, 0, Q_TILE, False) for d in range(per_tile)])])

    def pair_body(j, _):
        tiles = (2 * j, 2 * j + 1)
        starts = [pl.multiple_of(i * Q_TILE, Q_TILE) for i in tiles]
        qbs = [q_ref[pl.ds(q0, Q_TILE), :] for q0 in starts]
        acc_ref[...] = jnp.zeros_like(acc_ref)
        carry_ref[...] = jnp.zeros_like(carry_ref)

        @pl.when(j == 0)
        def _():
            visit([(qbs[0], 0, first_blocks(tiles[0], False)), (qbs[1], 1, first_blocks(tiles[1], True))])

        @pl.when(j > 0)
        def _():
            visit([(qb, slot, first_blocks(i, True)) for slot, (i, qb) in enumerate(zip(tiles, qbs))])

        live = jnp.maximum(jnp.where(j > 0, carry_ref[0], carry_ref[1]), carry_ref[1])

        @pl.when(jnp.max(live) > EXP_UNDERFLOW_LOG)
        def _():
            for slot, (i, qb) in enumerate(zip(tiles, qbs)):
                walk_on(i, qb, slot)

        for slot, q0 in enumerate(starts):
            o_ref[pl.ds(q0, Q_TILE), :] = acc_ref[slot].astype(o_ref.dtype)
        return 0

    lax.fori_loop(0, seq // (2 * Q_TILE), pair_body, 0)


def _attention(q, k, v):
    b, s, _ = q.shape
    nb = s // KEY_BLOCK
    spec = pl.BlockSpec((None, s, LANES), lambda bi, j: (bi, 0, j))
    return pl.pallas_call(
        functools.partial(_attn_kernel, seq=s),
        grid=(b, SB_WIDTH // LANES),
        in_specs=[spec, spec, spec],
        out_specs=spec,
        out_shape=jax.ShapeDtypeStruct((b, s, SB_WIDTH), BF16),
        scratch_shapes=[
            pltpu.VMEM((nb, LANES, 2 * KEY_BLOCK), BF16),
            pltpu.VMEM((nb, 2 * KEY_BLOCK, LANES), BF16),
            pltpu.VMEM((4 * KEY_BLOCK, 2 * KEY_BLOCK), BF16),
            pltpu.VMEM((2, Q_TILE, LANES), F32),
            pltpu.VMEM((2, Q_TILE, LANES), F32),
        ],
        compiler_params=_params(2),
        name="stickbreak_attn",
    )(q, k, v)


def _mix0_kernel(a_ref, u_ref, uh_ref, x_ref, pw_ref, ps_ref, wo_ref, g_ref, b_ref, o_ref,
                 ubuf, s2, s4, s8, s16_ref):
    i = pl.program_id(1)
    ts = u_ref.shape[0]
    n = HALO + ts
    gd = POOL_GROUP_DIM
    ubuf[0:HALO, :] = jnp.where(i > 0, uh_ref[...], 0.0)
    ubuf[HALO:, :] = u_ref[...]
    s2[8:n, :] = ubuf[8:n, :] + ubuf[7:n - 1, :]
    s4[16:n, gd:] = s2[16:n, gd:] + s2[14:n - 2, gd:]
    s8[24:n, 2 * gd:] = s4[24:n, 2 * gd:] + s4[20:n - 4, 2 * gd:]
    s16_ref[HALO:n, 3 * gd:] = s8[HALO:n, 3 * gd:] + s8[HALO - 8:n - 8, 3 * gd:]
    window_sums = (s2, s4, s8, s16_ref)

    def project(s, _):
        rows = slice(s * SUB_ROWS, (s + 1) * SUB_ROWS)
        held = slice(HALO + s * SUB_ROWS, HALO + (s + 1) * SUB_ROWS)
        pos1 = (i * ts + s * SUB_ROWS + 1
                + lax.broadcasted_iota(jnp.int32, (SUB_ROWS, gd), 0)).astype(F32)
        mixed = []
        for g, w in enumerate(POOL_WINDOWS):
            cols = slice(g * gd, (g + 1) * gd)
            pooled = window_sums[g][held, cols] / jnp.minimum(pos1, float(w)) - u_ref[rows, cols]
            mixed.append((_dot(pooled.astype(BF16), pw_ref[g]) * ps_ref[:, cols]).astype(BF16))
        cat = jnp.concatenate([a_ref[rows, :]] + mixed, axis=1)
        return _dot(cat, wo_ref[...])

    def normalize(s, mix):
        rows = slice(s * SUB_ROWS, (s + 1) * SUB_ROWS)
        o_ref[rows, :] = _layer_norm(ALPHA * x_ref[rows, :] + mix, g_ref[...], b_ref[...])

    _skewed(ts // SUB_ROWS, [project, normalize])


def _mix0(a, u, x, pool_w, pool_scale, w_out, ln_g, ln_b):
    b, s, _ = x.shape
    ts = ROW_TILE
    tile = lambda bi, i: (bi, i, 0)
    halo = lambda bi, i: (bi, jnp.maximum(i * (ts // HALO) - 1, 0), 0)
    return pl.pallas_call(
        _mix0_kernel,
        grid=(b, s // ts),
        in_specs=[
            pl.BlockSpec((None, ts, SB_WIDTH), tile),
            pl.BlockSpec((None, ts, POOL_WIDTH), tile),
            pl.BlockSpec((None, HALO, POOL_WIDTH), halo),
            pl.BlockSpec((None, ts, D_MODEL), tile),
            _const_spec(pool_w.shape), _const_spec(pool_scale.shape), _const_spec(w_out.shape),
            _const_spec(ln_g.shape), _const_spec(ln_b.shape),
        ],
        out_specs=pl.BlockSpec((None, ts, D_MODEL), tile),
        out_shape=jax.ShapeDtypeStruct((b, s, D_MODEL), F32),
        scratch_shapes=[pltpu.VMEM((HALO + ts, POOL_WIDTH), F32)] * 5,
        compiler_params=_params(2),
        name="mix0",
    )(a, u, u, x, pool_w, pool_scale, w_out, ln_g, ln_b)


def _ffn_kernel(x_ref, p_ref, wg_ref, wu_ref, wd_ref, g_ref, b_ref, wpg_ref, bpg_ref, wpp_ref, o_ref,
                h_ref):
    def rows(s):
        return slice(s * SUB_ROWS, (s + 1) * SUB_ROWS)

    def swiglu(s, _):
        xb = x_ref[rows(s), :].astype(BF16)
        for c in range(D_FF // FF_CHUNK):
            cols = slice(c * FF_CHUNK, (c + 1) * FF_CHUNK)
            gate = _dot(xb, wg_ref[:, cols])
            up = _dot(xb, wu_ref[:, cols])
            h_ref[rows(s), cols] = (jax.nn.silu(gate) * up).astype(BF16)
        return _dot(h_ref[rows(s), :], wd_ref[...])

    def norm_and_embed(s, y):
        x2 = _layer_norm(ALPHA * x_ref[rows(s), :] + y, g_ref[...], b_ref[...])
        ple_gate = jax.nn.sigmoid(_dot(x2.astype(BF16), wpg_ref[...]) + bpg_ref[...])
        o_ref[rows(s), :] = x2 + ple_gate * _dot(p_ref[rows(s), :].astype(BF16), wpp_ref[...])

    _skewed(x_ref.shape[0] // SUB_ROWS, [swiglu, norm_and_embed])


def _ffn(x2d, p3d, layer, w_gate, w_up, w_down, ln_g, ln_b, w_pg, b_pg, w_pp):
    t = x2d.shape[0]
    row = lambda i: (i, 0)
    consts = (w_gate, w_up, w_down, ln_g, ln_b, w_pg, b_pg, w_pp)
    return pl.pallas_call(
        _ffn_kernel,
        grid=(t // ROW_TILE,),
        in_specs=[pl.BlockSpec((ROW_TILE, D_MODEL), row),
                  pl.BlockSpec((None, ROW_TILE, PLE_DIM), lambda i: (layer, i, 0))]
        + [_const_spec(c.shape) for c in consts],
        out_specs=pl.BlockSpec((ROW_TILE, D_MODEL), row),
        out_shape=jax.ShapeDtypeStruct((t, D_MODEL), F32),
        scratch_shapes=[pltpu.VMEM((ROW_TILE, D_FF), BF16)],
        compiler_params=_params(1),
        name="ffn_ple",
    )(x2d, p3d, *consts)


def _inproj1_kernel(x_ref, w_ref, g_ref, b_ref, hc_ref, u_ref, v_ref):
    c = CONV_WIDTH

    def rows(s):
        return slice(s * SUB_ROWS, (s + 1) * SUB_ROWS)

    def project(s, _):
        xb = x_ref[rows(s), :].astype(BF16)
        return [_dot(xb, w_ref[:, lo:lo + c]) for lo in (0, c, 2 * c, 2 * c + SG_WIDTH)]

    def activate(s, h):
        a, gl, zu, zv = h
        hc_ref[rows(s), :] = a * jax.nn.sigmoid(gl)
        u_ref[rows(s), :] = jax.nn.gelu(zu)
        v_ref[rows(s), :] = _layer_norm(jax.nn.gelu(zv), g_ref[...], b_ref[...]).astype(BF16)

    _skewed(x_ref.shape[0] // SUB_ROWS, [project, activate])


def _inproj1(x2d, w_in, sg_ln_g, sg_ln_b):
    t = x2d.shape[0]
    row = lambda i: (i, 0)
    return pl.pallas_call(
        _inproj1_kernel,
        grid=(t // ROW_TILE,),
        in_specs=[pl.BlockSpec((ROW_TILE, D_MODEL), row), _const_spec(w_in.shape),
                  _const_spec(sg_ln_g.shape), _const_spec(sg_ln_b.shape)],
        out_specs=[pl.BlockSpec((ROW_TILE, CONV_WIDTH), row)] * 3,
        out_shape=[jax.ShapeDtypeStruct((t, CONV_WIDTH), F32), jax.ShapeDtypeStruct((t, SG_WIDTH), F32),
                   jax.ShapeDtypeStruct((t, SG_WIDTH), BF16)],
        compiler_params=_params(1),
        name="inproj1",
    )(x2d, w_in, sg_ln_g, sg_ln_b)


def _mix1_kernel(hc_ref, hh_ref, u_ref, v_ref, x_ref, dw_ref, cg_ref, cb_ref, sw_ref, sb_ref, wo_ref,
                 g_ref, b_ref, o_ref, shift_ref, conv_ref, cat_ref):
    i = pl.program_id(1)
    ts = hc_ref.shape[0]
    n = HALO + ts
    shift_ref[0, 0:HALO, :] = jnp.where(i > 0, hh_ref[...], 0.0)
    shift_ref[0, HALO:, :] = hc_ref[...]
    for r in range(1, SUBLANES):
        shift_ref[r, 0:n - SUBLANES, :] = shift_ref[0, r:r + n - SUBLANES, :]

    tt = lax.broadcasted_iota(jnp.int32, (SG_CHUNK, SG_CHUNK), 0)
    ss = lax.broadcasted_iota(jnp.int32, (SG_CHUNK, SG_CHUNK), 1)
    gate_w = [jnp.where(ss <= tt, sw_ref[g], 0.0).astype(BF16) for g in range(SG_GROUPS)]
    first = HALO - (CONV_TAPS - 1)

    def mixers(s, _):
        base = s * SUB_ROWS
        rows = slice(base, base + SUB_ROWS)
        for lb in range(CONV_WIDTH // LANES):
            cols = slice(lb * LANES, (lb + 1) * LANES)
            for rb in range(SUB_ROWS // CONV_ROWS):
                acc = jnp.zeros((CONV_ROWS, LANES), F32)
                for kk in range(CONV_TAPS):
                    r0 = base + rb * CONV_ROWS + (first + kk) // SUBLANES * SUBLANES
                    acc = acc + (dw_ref[kk:kk + 1, cols]
                                 * shift_ref[(first + kk) % SUBLANES, r0:r0 + CONV_ROWS, cols])
                conv_ref[base + rb * CONV_ROWS:base + (rb + 1) * CONV_ROWS, cols] = acc
        cat_ref[rows, 0:CONV_WIDTH] = jax.nn.silu(
            _layer_norm(conv_ref[rows, :], cg_ref[...], cb_ref[...])).astype(BF16)
        chunks = [slice(base + c * SG_CHUNK, base + (c + 1) * SG_CHUNK) for c in range(SUB_ROWS // SG_CHUNK)]
        for g in range(SG_GROUPS):
            cols = slice(g * SG_GROUP_DIM, (g + 1) * SG_GROUP_DIM)
            sv = _dot(gate_w[g], jnp.concatenate([v_ref[ch, cols] for ch in chunks], axis=1))
            for c, ch in enumerate(chunks):
                gated = u_ref[ch, cols] * (sv[:, c * SG_GROUP_DIM:(c + 1) * SG_GROUP_DIM] + sb_ref[:, cols])
                cat_ref[ch, CONV_WIDTH + g * SG_GROUP_DIM:CONV_WIDTH + (g + 1) * SG_GROUP_DIM] = gated.astype(BF16)

    def project(s, _):
        return _dot(cat_ref[s * SUB_ROWS:(s + 1) * SUB_ROWS, :], wo_ref[...])

    def normalize(s, mix):
        rows = slice(s * SUB_ROWS, (s + 1) * SUB_ROWS)
        o_ref[rows, :] = _layer_norm(ALPHA * x_ref[rows, :] + mix, g_ref[...], b_ref[...])

    _skewed(ts // SUB_ROWS, [mixers, project, normalize])


def _mix1(hc, u, v, x, conv_dw, conv_g, conv_b, sg_w, sg_bias, w_out, ln_g, ln_b):
    b, s, _ = x.shape
    ts = CONV_TILE
    tile = lambda bi, i: (bi, i, 0)
    halo = lambda bi, i: (bi, jnp.maximum(i * (ts // HALO) - 1, 0), 0)
    consts = (conv_dw, conv_g, conv_b, sg_w, sg_bias, w_out, ln_g, ln_b)
    return pl.pallas_call(
        _mix1_kernel,
        grid=(b, s // ts),
        in_specs=[
            pl.BlockSpec((None, ts, CONV_WIDTH), tile),
            pl.BlockSpec((None, HALO, CONV_WIDTH), halo),
            pl.BlockSpec((None, ts, SG_WIDTH), tile),
            pl.BlockSpec((None, ts, SG_WIDTH), tile),
            pl.BlockSpec((None, ts, D_MODEL), tile),
        ] + [_const_spec(c.shape) for c in consts],
        out_specs=pl.BlockSpec((None, ts, D_MODEL), tile),
        out_shape=jax.ShapeDtypeStruct((b, s, D_MODEL), F32),
        scratch_shapes=[pltpu.VMEM((SUBLANES, HALO + ts, CONV_WIDTH), F32), pltpu.VMEM((ts, CONV_WIDTH), F32),
                        pltpu.VMEM((ts, MIX_WIDTH), BF16)],
        compiler_params=_params(2),
        name="mix1",
    )(hc, hc, u, v, x, *consts)


def _row(vec):
    return vec.reshape(1, -1)


def kernel(x, p, even_w_in, even_w_out, pool_w, pool_scale, odd_w_in, odd_w_out, conv_dw, conv_ln_g,
           conv_ln_b, sg_ln_g, sg_ln_b, sg_w, sg_b, ln_mix_g, ln_mix_b, ffn_w_gate, ffn_w_up, ffn_w_down,
           ln_ffn_g, ln_ffn_b, ple_w_proj, ple_w_gate, ple_b_gate):
    b, s, d = x.shape
    t = b * s
    bf = lambda w: w.astype(BF16)

    p3d = p.reshape(DEPTH, t, PLE_DIM)

    def ffn(xin, i):
        return _ffn(xin.reshape(t, d), p3d, i, bf(ffn_w_gate[i]), bf(ffn_w_up[i]),
                    bf(ffn_w_down[i]), _row(ln_ffn_g[i]), _row(ln_ffn_b[i]), bf(ple_w_gate[i]),
                    _row(ple_b_gate[i]), bf(ple_w_proj[i])).reshape(b, s, d)

    q, k, v, u = _inproj0(x.reshape(t, d), bf(even_w_in[0]))
    attn = _attention(q.reshape(b, s, SB_WIDTH), k.reshape(b, s, SB_WIDTH), v.reshape(b, s, SB_WIDTH))
    x = _mix0(attn, u.reshape(b, s, POOL_WIDTH), x, bf(pool_w[0]), _row(pool_scale[0]), bf(even_w_out[0]),
              _row(ln_mix_g[0]), _row(ln_mix_b[0]))
    x = ffn(x, 0)

    hc, su, sv = _inproj1(x.reshape(t, d), bf(odd_w_in[0]), _row(sg_ln_g[0]), _row(sg_ln_b[0]))
    sg_bias = jnp.repeat(sg_b[0].T, SG_GROUP_DIM, axis=1)
    x = _mix1(hc.reshape(b, s, CONV_WIDTH), su.reshape(b, s, SG_WIDTH), sv.reshape(b, s, SG_WIDTH), x,
              conv_dw[0], _row(conv_ln_g[0]), _row(conv_ln_b[0]), sg_w[0], sg_bias, bf(odd_w_out[0]),
              _row(ln_mix_g[1]), _row(ln_mix_b[1]))
    return ffn(x, 1)
```

```python
import functools

import jax
import jax.numpy as jnp
from jax import lax
from jax.experimental import pallas as pl
from jax.experimental.pallas import tpu as pltpu

F32 = jnp.float32
BF16 = jnp.bfloat16

D_MODEL = 1024
N_SB_HEADS = 8
SB_HEAD_DIM = 64
SB_WIDTH = N_SB_HEADS * SB_HEAD_DIM
POOL_WINDOWS = (2, 4, 8, 16)
POOL_GROUP_DIM = 128
POOL_WIDTH = len(POOL_WINDOWS) * POOL_GROUP_DIM
CONV_WIDTH = 512
CONV_TAPS = 31
SG_GROUPS = 4
SG_GROUP_DIM = 128
SG_WIDTH = SG_GROUPS * SG_GROUP_DIM
SG_CHUNK = 128
MIX_WIDTH = 1024
D_FF = 2816
PLE_DIM = 256
DEPTH = 2
ALPHA = (2 * DEPTH) ** 0.25
LN_EPS = 1e-5

LANES = 128
SUBLANES = 8
MXU_DIM = 256
VMEM_LIMIT = 56 * 1024 * 1024

ROW_TILE = 1024
SUB_ROWS = 256
HALO = 32
KEY_BLOCK = 128
Q_TILE = 256
EXP_UNDERFLOW_LOG = -104.0
FF_CHUNK = 256
CONV_ROWS = 64


def _layer_norm(y, g, b):
    mu = jnp.mean(y, axis=-1, keepdims=True)
    d = y - mu
    var = jnp.mean(d * d, axis=-1, keepdims=True)
    return d * lax.rsqrt(var + LN_EPS) * g + b


def _dot(a, b):
    return jnp.dot(a, b, preferred_element_type=F32)


def _const_spec(shape):
    zeros = (0,) * len(shape)
    return pl.BlockSpec(shape, lambda *_: zeros, pipeline_mode=pl.Buffered(1))


def _params(n_axes):
    return pltpu.CompilerParams(dimension_semantics=("arbitrary",) * n_axes,
                                vmem_limit_bytes=VMEM_LIMIT)


def _skewed(n_sub, stages):
    values = [None] * n_sub
    for step in range(n_sub + len(stages) - 1):
        for k, stage in enumerate(stages):
            s = step - k
            if 0 <= s < n_sub:
                values[s] = stage(s, values[s])


def _inproj0_kernel(x_ref, w_ref, q_ref, k_ref, v_ref, u_ref):
    w = SB_WIDTH
    for s in range(x_ref.shape[0] // SUB_ROWS):
        rows = slice(s * SUB_ROWS, (s + 1) * SUB_ROWS)
        xb = x_ref[rows, :].astype(BF16)
        q_ref[rows, :] = (_dot(xb, w_ref[:, 0:w]) * (SB_HEAD_DIM ** -0.5)).astype(BF16)
        k_ref[rows, :] = _dot(xb, w_ref[:, w:2 * w]).astype(BF16)
        v_ref[rows, :] = _dot(xb, w_ref[:, 2 * w:3 * w]).astype(BF16)
        u_ref[rows, :] = _dot(xb, w_ref[:, 3 * w:])


def _inproj0(x2d, w_in):
    t = x2d.shape[0]
    row = lambda i: (i, 0)
    return pl.pallas_call(
        _inproj0_kernel,
        grid=(t // ROW_TILE,),
        in_specs=[pl.BlockSpec((ROW_TILE, D_MODEL), row), _const_spec(w_in.shape)],
        out_specs=[pl.BlockSpec((ROW_TILE, SB_WIDTH), row)] * 3 + [pl.BlockSpec((ROW_TILE, POOL_WIDTH), row)],
        out_shape=[jax.ShapeDtypeStruct((t, SB_WIDTH), BF16)] * 3 + [jax.ShapeDtypeStruct((t, POOL_WIDTH), F32)],
        compiler_params=_params(1),
        name="inproj0",
    )(x2d, w_in)


def _attn_kernel(q_ref, k_ref, v_ref, o_ref, kt_ref, vcat_ref, sfx_ref, acc_ref, carry_ref, *, seq):
    nb = seq // KEY_BLOCK
    kb_cols = 2 * KEY_BLOCK
    half = LANES // 2
    per_tile = Q_TILE // KEY_BLOCK

    kt = k_ref[...].astype(F32).T
    dim = lax.broadcasted_iota(jnp.int32, kt.shape, 0)
    kt_a = jnp.where(dim < half, kt, 0.0).astype(BF16)
    kt_b = jnp.where(dim < half, 0.0, kt).astype(BF16)
    for kb in range(nb):
        kt_ref[kb, :, 0:KEY_BLOCK] = kt_a[:, kb * KEY_BLOCK:(kb + 1) * KEY_BLOCK]
        kt_ref[kb, :, KEY_BLOCK:] = kt_b[:, kb * KEY_BLOCK:(kb + 1) * KEY_BLOCK]
    lane = lax.broadcasted_iota(jnp.int32, (seq, LANES), 1)
    v = v_ref[...]
    zero = jnp.zeros_like(v)
    vcat_ref[:, 0:KEY_BLOCK, :] = jnp.where(lane < half, v, zero).reshape(nb, KEY_BLOCK, LANES)
    vcat_ref[:, KEY_BLOCK:, :] = jnp.where(lane < half, zero, v).reshape(nb, KEY_BLOCK, LANES)

    rj = lax.broadcasted_iota(jnp.int32, (2 * kb_cols, kb_cols), 0)
    cs = lax.broadcasted_iota(jnp.int32, (2 * kb_cols, kb_cols), 1)
    rj = jnp.where(rj >= kb_cols, rj - kb_cols, rj)
    in_suffix = ((rj < KEY_BLOCK) == (cs < KEY_BLOCK)) & (rj >= cs)
    sfx_ref[...] = jnp.where(in_suffix, -1.0, 0.0).astype(BF16)
    out_is_a = lax.broadcasted_iota(jnp.int32, (KEY_BLOCK, LANES), 1) < half

    row = lax.broadcasted_iota(jnp.int32, (KEY_BLOCK, kb_cols), 0)
    col = lax.broadcasted_iota(jnp.int32, (KEY_BLOCK, kb_cols), 1)
    on_diagonal = jnp.where(col >= KEY_BLOCK, col - KEY_BLOCK, col) < row
    halves = (slice(0, KEY_BLOCK), slice(KEY_BLOCK, Q_TILE))

    def mask_diagonal(val):
        top = jnp.where(on_diagonal, val[0:KEY_BLOCK], 0.0)
        return top if val.shape[0] == KEY_BLOCK else jnp.concatenate([top, val[KEY_BLOCK:]], axis=0)

    def visit(jobs):
        flat = [(qb, slot, blk) for qb, slot, blocks in jobs for blk in blocks]
        zs = [_dot(qb[r0:r0 + nr], kt_ref[kb]) for qb, _, (kb, r0, nr, _) in flat]
        sps = []
        for z, (_, _, blk) in zip(zs, flat):
            sp = jnp.maximum(z, 0.0) + jnp.log(1.0 + jnp.exp(-jnp.abs(z)))
            sps.append(mask_diagonal(sp) if blk[3] else sp)
        his = [sp.astype(BF16) for sp in sps]
        los = [(sp - hi.astype(F32)).astype(BF16) for sp, hi in zip(sps, his)]
        sums = [_dot(jnp.concatenate([hi, lo], axis=1), sfx_ref[...]) for hi, lo in zip(his, los)]
        pvs = []
        for s, z, (_, _, (kb, _, _, diagonal)) in zip(sums, zs, flat):
            w = jnp.exp(z + s)
            if diagonal:
                w = mask_diagonal(w)
            pvs.append(_dot(w.astype(BF16), vcat_ref[kb]))
        for _, slot, _ in jobs:
            acc = [acc_ref[slot, h, :] for h in halves]
            carry = [carry_ref[slot, h, :] for h in halves]
            for pv, sp, (_, blk_slot, (_, r0, nr, _)) in zip(pvs, sps, flat):
                if blk_slot != slot:
                    continue
                tot_a = jnp.sum(sp[:, 0:KEY_BLOCK], axis=1, keepdims=True)
                tot_b = jnp.sum(sp[:, KEY_BLOCK:], axis=1, keepdims=True)
                for h, half in enumerate(halves):
                    if r0 <= half.start and half.stop <= r0 + nr:
                        mine = slice(half.start - r0, half.stop - r0)
                        acc[h] = acc[h] + jnp.exp(carry[h]) * pv[mine]
                        carry[h] = carry[h] - jnp.where(out_is_a, tot_a[mine], tot_b[mine])
            for h, half in enumerate(halves):
                acc_ref[slot, half, :] = acc[h]
                carry_ref[slot, half, :] = carry[h]

    def first_blocks(i, with_previous):
        kd = i * per_tile
        blocks = [(kd + 1, KEY_BLOCK, KEY_BLOCK, True), (kd, 0, Q_TILE, True)]
        if with_previous:
            blocks += [(kd - 1, 0, Q_TILE, False), (kd - 2, 0, KEY_BLOCK, False)]
        return blocks

    def walk_on(i, qb, slot):
        def max_carry():
            return jnp.max(carry_ref[slot])

        kd = i * per_tile

        @pl.when((i > 0) & (max_carry() > EXP_UNDERFLOW_LOG))
        def _():
            visit([(qb, slot, [(kd - 2, KEY_BLOCK, KEY_BLOCK, False)])])

        left = jnp.maximum(i - 1, 0) * per_tile
        trips = left // (2 * per_tile)

        def k_cond(state):
            n, top = state
            return (n < trips) & (top > EXP_UNDERFLOW_LOG)

        def k_body(state):
            n, _ = state
            first = left - 1 - n * 2 * per_tile
            visit([(qb, slot, [(first - d, 0, Q_TILE, False) for d in range(2 * per_tile)])])
            return n + 1, max_carry()

        n_done, top = lax.while_loop(k_cond, k_body, (0, max_carry()))

        @pl.when((n_done == trips) & (left - trips * 2 * per_tile > 0) & (top > EXP_UNDERFLOW_LOG))
        def _():
            visit([(qb, slot, [(per_tile - 1 - d, 0, Q_TILE, False) for d in range(per_tile)])])

    def pair_body(j, _):
        tiles = (2 * j, 2 * j + 1)
        starts = [pl.multiple_of(i * Q_TILE, Q_TILE) for i in tiles]
        qbs = [q_ref[pl.ds(q0, Q_TILE), :] for q0 in starts]
        acc_ref[...] = jnp.zeros_like(acc_ref)
        carry_ref[...] = jnp.zeros_like(carry_ref)

        @pl.when(j == 0)
        def _():
            visit([(qbs[0], 0, first_blocks(tiles[0], False)), (qbs[1], 1, first_blocks(tiles[1], True))])

        @pl.when(j > 0)
        def _():
            visit([(qb, slot, first_blocks(i, True)) for slot, (i, qb) in enumerate(zip(tiles, qbs))])

        live = jnp.maximum(jnp.where(j > 0, carry_ref[0], carry_ref[1]), carry_ref[1])

        @pl.when(jnp.max(live) > EXP_UNDERFLOW_LOG)
        def _():
            for slot, (i, qb) in enumerate(zip(tiles, qbs)):
                walk_on(i, qb, slot)

        for slot, q0 in enumerate(starts):
            o_ref[pl.ds(q0, Q_TILE), :] = acc_ref[slot].astype(o_ref.dtype)
        return 0

    lax.fori_loop(0, seq // (2 * Q_TILE), pair_body, 0)


def _attention(q, k, v):
    b, s, _ = q.shape
    nb = s // KEY_BLOCK
    spec = pl.BlockSpec((None, s, LANES), lambda bi, j: (bi, 0, j))
    return pl.pallas_call(
        functools.partial(_attn_kernel, seq=s),
        grid=(b, SB_WIDTH // LANES),
        in_specs=[spec, spec, spec],
        out_specs=spec,
        out_shape=jax.ShapeDtypeStruct((b, s, SB_WIDTH), BF16),
        scratch_shapes=[
            pltpu.VMEM((nb, LANES, 2 * KEY_BLOCK), BF16),
            pltpu.VMEM((nb, 2 * KEY_BLOCK, LANES), BF16),
            pltpu.VMEM((4 * KEY_BLOCK, 2 * KEY_BLOCK), BF16),
            pltpu.VMEM((2, Q_TILE, LANES), F32),
            pltpu.VMEM((2, Q_TILE, LANES), F32),
        ],
        compiler_params=_params(2),
        name="stickbreak_attn",
    )(q, k, v)


def _mix0_kernel(a_ref, u_ref, uh_ref, x_ref, pw_ref, ps_ref, wo_ref, g_ref, b_ref, o_ref,
                 ubuf, s2, s4, s8, s16_ref):
    i = pl.program_id(1)
    ts = u_ref.shape[0]
    n = HALO + ts
    gd = POOL_GROUP_DIM
    ubuf[0:HALO, :] = jnp.where(i > 0, uh_ref[...], 0.0)
    ubuf[HALO:, :] = u_ref[...]
    s2[8:n, :] = ubuf[8:n, :] + ubuf[7:n - 1, :]
    s4[16:n, gd:] = s2[16:n, gd:] + s2[14:n - 2, gd:]
    s8[24:n, 2 * gd:] = s4[24:n, 2 * gd:] + s4[20:n - 4, 2 * gd:]
    s16_ref[HALO:n, 3 * gd:] = s8[HALO:n, 3 * gd:] + s8[HALO - 8:n - 8, 3 * gd:]
    window_sums = (s2, s4, s8, s16_ref)

    def project(s, _):
        rows = slice(s * SUB_ROWS, (s + 1) * SUB_ROWS)
        held = slice(HALO + s * SUB_ROWS, HALO + (s + 1) * SUB_ROWS)
        pos1 = (i * ts + s * SUB_ROWS + 1
                + lax.broadcasted_iota(jnp.int32, (SUB_ROWS, gd), 0)).astype(F32)
        mixed = []
        for g, w in enumerate(POOL_WINDOWS):
            cols = slice(g * gd, (g + 1) * gd)
            pooled = window_sums[g][held, cols] / jnp.minimum(pos1, float(w)) - u_ref[rows, cols]
            mixed.append((_dot(pooled.astype(BF16), pw_ref[g]) * ps_ref[:, cols]).astype(BF16))
        cat = jnp.concatenate([a_ref[rows, :]] + mixed, axis=1)
        return _dot(cat, wo_ref[...])

    def normalize(s, mix):
        rows = slice(s * SUB_ROWS, (s + 1) * SUB_ROWS)
        o_ref[rows, :] = _layer_norm(ALPHA * x_ref[rows, :] + mix, g_ref[...], b_ref[...])

    _skewed(ts // SUB_ROWS, [project, normalize])


def _mix0(a, u, x, pool_w, pool_scale, w_out, ln_g, ln_b):
    b, s, _ = x.shape
    ts = ROW_TILE
    tile = lambda bi, i: (bi, i, 0)
    halo = lambda bi, i: (bi, jnp.maximum(i * (ts // HALO) - 1, 0), 0)
    return pl.pallas_call(
        _mix0_kernel,
        grid=(b, s // ts),
        in_specs=[
            pl.BlockSpec((None, ts, SB_WIDTH), tile),
            pl.BlockSpec((None, ts, POOL_WIDTH), tile),
            pl.BlockSpec((None, HALO, POOL_WIDTH), halo),
            pl.BlockSpec((None, ts, D_MODEL), tile),
            _const_spec(pool_w.shape), _const_spec(pool_scale.shape), _const_spec(w_out.shape),
            _const_spec(ln_g.shape), _const_spec(ln_b.shape),
        ],
        out_specs=pl.BlockSpec((None, ts, D_MODEL), tile),
        out_shape=jax.ShapeDtypeStruct((b, s, D_MODEL), F32),
        scratch_shapes=[pltpu.VMEM((HALO + ts, POOL_WIDTH), F32)] * 5,
        compiler_params=_params(2),
        name="mix0",
    )(a, u, u, x, pool_w, pool_scale, w_out, ln_g, ln_b)


def _ffn_kernel(x_ref, p_ref, wg_ref, wu_ref, wd_ref, g_ref, b_ref, wpg_ref, bpg_ref, wpp_ref, o_ref,
                h_ref):
    def rows(s):
        return slice(s * SUB_ROWS, (s + 1) * SUB_ROWS)

    def swiglu(s, _):
        xb = x_ref[rows(s), :].astype(BF16)
        for c in range(D_FF // FF_CHUNK):
            cols = slice(c * FF_CHUNK, (c + 1) * FF_CHUNK)
            gate = _dot(xb, wg_ref[:, cols])
            up = _dot(xb, wu_ref[:, cols])
            h_ref[rows(s), cols] = (jax.nn.silu(gate) * up).astype(BF16)
        return _dot(h_ref[rows(s), :], wd_ref[...])

    def norm_and_embed(s, y):
        x2 = _layer_norm(ALPHA * x_ref[rows(s), :] + y, g_ref[...], b_ref[...])
        ple_gate = jax.nn.sigmoid(_dot(x2.astype(BF16), wpg_ref[...]) + bpg_ref[...])
        o_ref[rows(s), :] = x2 + ple_gate * _dot(p_ref[rows(s), :].astype(BF16), wpp_ref[...])

    _skewed(x_ref.shape[0] // SUB_ROWS, [swiglu, norm_and_embed])


def _ffn(x2d, p3d, layer, w_gate, w_up, w_down, ln_g, ln_b, w_pg, b_pg, w_pp):
    t = x2d.shape[0]
    row = lambda i: (i, 0)
    consts = (w_gate, w_up, w_down, ln_g, ln_b, w_pg, b_pg, w_pp)

    def layer_spec(arr):
        return pl.BlockSpec((None,) + arr.shape[1:], lambda i: (layer, 0, 0), pipeline_mode=pl.Buffered(1))

    return pl.pallas_call(
        _ffn_kernel,
        grid=(t // ROW_TILE,),
        in_specs=[pl.BlockSpec((ROW_TILE, D_MODEL), row),
                  pl.BlockSpec((None, ROW_TILE, PLE_DIM), lambda i: (layer, i, 0))]
        + [layer_spec(c) for c in consts],
        out_specs=pl.BlockSpec((ROW_TILE, D_MODEL), row),
        out_shape=jax.ShapeDtypeStruct((t, D_MODEL), F32),
        scratch_shapes=[pltpu.VMEM((ROW_TILE, D_FF), BF16)],
        compiler_params=_params(1),
        name="ffn_ple",
    )(x2d, p3d, *consts)


def _layer1_kernel(x_ref, wi_ref, sg_g_ref, sg_b_ref, dw_ref, cg_ref, cb_ref, sw_ref, sb_ref, wo_ref,
                   g_ref, b_ref, o_ref, hc_ref, shift_ref, conv_ref, cat_ref, u_ref, v_ref):
    i = pl.program_id(1)
    ts = x_ref.shape[0]
    c = CONV_WIDTH
    span = SUB_ROWS + HALO - SUBLANES

    @pl.when(i == 0)
    def _():
        hc_ref[0:HALO, :] = jnp.zeros((HALO, c), F32)

    tt = lax.broadcasted_iota(jnp.int32, (SG_CHUNK, SG_CHUNK), 0)
    ss = lax.broadcasted_iota(jnp.int32, (SG_CHUNK, SG_CHUNK), 1)
    gate_w = [jnp.where(ss <= tt, sw_ref[g], 0.0).astype(BF16) for g in range(SG_GROUPS)]
    first = HALO - (CONV_TAPS - 1)

    def rows_of(s):
        return slice(s * SUB_ROWS, (s + 1) * SUB_ROWS)

    def project_in(s, _):
        xb = x_ref[rows_of(s), :].astype(BF16)
        return [_dot(xb, wi_ref[:, lo:lo + c]) for lo in (0, c, 2 * c, 2 * c + SG_WIDTH)]

    def activate(s, h):
        a, gl, zu, zv = h
        hc_ref[HALO + s * SUB_ROWS:HALO + (s + 1) * SUB_ROWS, :] = a * jax.nn.sigmoid(gl)
        u_ref[rows_of(s), :] = jax.nn.gelu(zu)
        v_ref[rows_of(s), :] = _layer_norm(jax.nn.gelu(zv), sg_g_ref[...], sg_b_ref[...]).astype(BF16)

    def mixers(s, _):
        base = s * SUB_ROWS
        rows = rows_of(s)
        for r in range(1, SUBLANES):
            shift_ref[r - 1, 0:span, :] = hc_ref[base + r:base + r + span, :]
        for lb in range(c // LANES):
            cols = slice(lb * LANES, (lb + 1) * LANES)
            for rb in range(SUB_ROWS // CONV_ROWS):
                acc = jnp.zeros((CONV_ROWS, LANES), F32)
                for kk in range(CONV_TAPS):
                    r, r0 = (first + kk) % SUBLANES, rb * CONV_ROWS + (first + kk) // SUBLANES * SUBLANES
                    taps = (hc_ref[base + r0:base + r0 + CONV_ROWS, cols] if r == 0
                            else shift_ref[r - 1, r0:r0 + CONV_ROWS, cols])
                    acc = acc + dw_ref[kk:kk + 1, cols] * taps
                conv_ref[base + rb * CONV_ROWS:base + (rb + 1) * CONV_ROWS, cols] = acc
        cat_ref[rows, 0:c] = jax.nn.silu(_layer_norm(conv_ref[rows, :], cg_ref[...], cb_ref[...])).astype(BF16)
        chunks = [slice(base + k * SG_CHUNK, base + (k + 1) * SG_CHUNK) for k in range(SUB_ROWS // SG_CHUNK)]
        for g in range(SG_GROUPS):
            cols = slice(g * SG_GROUP_DIM, (g + 1) * SG_GROUP_DIM)
            sv = _dot(gate_w[g], jnp.concatenate([v_ref[ch, cols] for ch in chunks], axis=1))
            for k, ch in enumerate(chunks):
                gated = u_ref[ch, cols] * (sv[:, k * SG_GROUP_DIM:(k + 1) * SG_GROUP_DIM] + sb_ref[:, cols])
                cat_ref[ch, c + g * SG_GROUP_DIM:c + (g + 1) * SG_GROUP_DIM] = gated.astype(BF16)

    def project_out(s, _):
        return _dot(cat_ref[rows_of(s), :], wo_ref[...])

    def normalize(s, mix):
        o_ref[rows_of(s), :] = _layer_norm(ALPHA * x_ref[rows_of(s), :] + mix, g_ref[...], b_ref[...])

    _skewed(ts // SUB_ROWS, [project_in, activate, mixers, project_out, normalize])
    hc_ref[0:HALO, :] = hc_ref[ts:ts + HALO, :]


def _layer1_mixers(x, w_in, sg_ln_g, sg_ln_b, conv_dw, conv_g, conv_b, sg_w, sg_bias, w_out, ln_g, ln_b):
    b, s, _ = x.shape
    ts = ROW_TILE
    tile = lambda bi, i: (bi, i, 0)
    consts = (w_in, sg_ln_g, sg_ln_b, conv_dw, conv_g, conv_b, sg_w, sg_bias, w_out, ln_g, ln_b)
    return pl.pallas_call(
        _layer1_kernel,
        grid=(b, s // ts),
        in_specs=[pl.BlockSpec((None, ts, D_MODEL), tile)] + [_const_spec(c.shape) for c in consts],
        out_specs=pl.BlockSpec((None, ts, D_MODEL), tile),
        out_shape=jax.ShapeDtypeStruct((b, s, D_MODEL), F32),
        scratch_shapes=[
            pltpu.VMEM((HALO + ts, CONV_WIDTH), F32),
            pltpu.VMEM((SUBLANES - 1, SUB_ROWS + HALO - SUBLANES, CONV_WIDTH), F32),
            pltpu.VMEM((ts, CONV_WIDTH), F32),
            pltpu.VMEM((ts, MIX_WIDTH), BF16),
            pltpu.VMEM((ts, SG_WIDTH), F32),
            pltpu.VMEM((ts, SG_WIDTH), BF16),
        ],
        compiler_params=_params(2),
        name="layer1_mixers",
    )(x, *consts)


def _row(vec):
    return vec.reshape(1, -1)


def kernel(x, p, even_w_in, even_w_out, pool_w, pool_scale, odd_w_in, odd_w_out, conv_dw, conv_ln_g,
           conv_ln_b, sg_ln_g, sg_ln_b, sg_w, sg_b, ln_mix_g, ln_mix_b, ffn_w_gate, ffn_w_up, ffn_w_down,
           ln_ffn_g, ln_ffn_b, ple_w_proj, ple_w_gate, ple_b_gate):
    b, s, d = x.shape
    t = b * s
    bf = lambda w: w.astype(BF16)

    p3d = p.reshape(DEPTH, t, PLE_DIM)
    rows = lambda m: m.reshape(DEPTH, 1, -1)
    ffn_params = (bf(ffn_w_gate), bf(ffn_w_up), bf(ffn_w_down), rows(ln_ffn_g), rows(ln_ffn_b),
                  bf(ple_w_gate), rows(ple_b_gate), bf(ple_w_proj))

    def ffn(xin, i):
        return _ffn(xin.reshape(t, d), p3d, i, *ffn_params).reshape(b, s, d)

    q, k, v, u = _inproj0(x.reshape(t, d), bf(even_w_in[0]))
    attn = _attention(q.reshape(b, s, SB_WIDTH), k.reshape(b, s, SB_WIDTH), v.reshape(b, s, SB_WIDTH))
    x = _mix0(attn, u.reshape(b, s, POOL_WIDTH), x, bf(pool_w[0]), _row(pool_scale[0]), bf(even_w_out[0]),
              _row(ln_mix_g[0]), _row(ln_mix_b[0]))
    x = ffn(x, 0)

    sg_bias = jnp.repeat(sg_b[0].T, SG_GROUP_DIM, axis=1)
    x = _layer1_mixers(x, bf(odd_w_in[0]), _row(sg_ln_g[0]), _row(sg_ln_b[0]), conv_dw[0], _row(conv_ln_g[0]),
                       _row(conv_ln_b[0]), sg_w[0], sg_bias, bf(odd_w_out[0]), _row(ln_mix_g[1]), _row(ln_mix_b[1]))
    return ffn(x, 1)
```

```python
import functools

import jax
import jax.numpy as jnp
from jax import lax
from jax.experimental import pallas as pl
from jax.experimental.pallas import tpu as pltpu

F32 = jnp.float32
BF16 = jnp.bfloat16

D_MODEL = 1024
N_SB_HEADS = 8
SB_HEAD_DIM = 64
SB_WIDTH = N_SB_HEADS * SB_HEAD_DIM
POOL_WINDOWS = (2, 4, 8, 16)
POOL_GROUP_DIM = 128
POOL_WIDTH = len(POOL_WINDOWS) * POOL_GROUP_DIM
CONV_WIDTH = 512
CONV_TAPS = 31
SG_GROUPS = 4
SG_GROUP_DIM = 128
SG_WIDTH = SG_GROUPS * SG_GROUP_DIM
SG_CHUNK = 128
MIX_WIDTH = 1024
D_FF = 2816
PLE_DIM = 256
DEPTH = 2
ALPHA = (2 * DEPTH) ** 0.25
LN_EPS = 1e-5

LANES = 128
SUBLANES = 8
MXU_DIM = 256
VMEM_LIMIT = 56 * 1024 * 1024

ROW_TILE = 1024
SUB_ROWS = 256
FUSED_TILE = 512
HALO = 32
KEY_BLOCK = 128
Q_TILE = 256
ROW_PART = 64
EXP_UNDERFLOW_LOG = -104.0
FF_CHUNK = 256
CONV_ROWS = 64


def _layer_norm(y, g, b):
    mu = jnp.mean(y, axis=-1, keepdims=True)
    d = y - mu
    var = jnp.mean(d * d, axis=-1, keepdims=True)
    return d * lax.rsqrt(var + LN_EPS) * g + b


def _dot(a, b):
    return jnp.dot(a, b, preferred_element_type=F32)


def _const_spec(shape):
    zeros = (0,) * len(shape)
    return pl.BlockSpec(shape, lambda *_: zeros, pipeline_mode=pl.Buffered(1))


def _layer_spec(arr, layer):
    index = (layer,) + (0,) * (arr.ndim - 1)
    return pl.BlockSpec((None,) + arr.shape[1:], lambda *_: index, pipeline_mode=pl.Buffered(1))


def _params(n_axes):
    return pltpu.CompilerParams(dimension_semantics=("arbitrary",) * n_axes,
                                vmem_limit_bytes=VMEM_LIMIT)


def _skewed(n_sub, stages):
    values = [None] * n_sub
    for step in range(n_sub + len(stages) - 1):
        for k, stage in enumerate(stages):
            s = step - k
            if 0 <= s < n_sub:
                values[s] = stage(s, values[s])


def _inproj0_kernel(x_ref, w_ref, q_ref, k_ref, v_ref, u_ref):
    w = SB_WIDTH
    for s in range(x_ref.shape[0] // SUB_ROWS):
        rows = slice(s * SUB_ROWS, (s + 1) * SUB_ROWS)
        xb = x_ref[rows, :].astype(BF16)
        q_ref[rows, :] = (_dot(xb, w_ref[:, 0:w]) * (SB_HEAD_DIM ** -0.5)).astype(BF16)
        k_ref[rows, :] = _dot(xb, w_ref[:, w:2 * w]).astype(BF16)
        v_ref[rows, :] = _dot(xb, w_ref[:, 2 * w:3 * w]).astype(BF16)
        u_ref[rows, :] = _dot(xb, w_ref[:, 3 * w:])


def _inproj0(x2d, w_in):
    t = x2d.shape[0]
    row = lambda i: (i, 0)
    return pl.pallas_call(
        _inproj0_kernel,
        grid=(t // ROW_TILE,),
        in_specs=[pl.BlockSpec((ROW_TILE, D_MODEL), row), _const_spec(w_in.shape)],
        out_specs=[pl.BlockSpec((ROW_TILE, SB_WIDTH), row)] * 3 + [pl.BlockSpec((ROW_TILE, POOL_WIDTH), row)],
        out_shape=[jax.ShapeDtypeStruct((t, SB_WIDTH), BF16)] * 3 + [jax.ShapeDtypeStruct((t, POOL_WIDTH), F32)],
        compiler_params=_params(1),
        name="inproj0",
    )(x2d, w_in)


def _attn_kernel(q_ref, k_ref, v_ref, o_ref, kt_ref, vcat_ref, sfx_ref, acc_ref, carry_ref, *, seq):
    nb = seq // KEY_BLOCK
    kb_cols = 2 * KEY_BLOCK
    half = LANES // 2
    per_tile = Q_TILE // KEY_BLOCK

    kt = k_ref[...].astype(F32).T
    dim = lax.broadcasted_iota(jnp.int32, kt.shape, 0)
    kt_a = jnp.where(dim < half, kt, 0.0).astype(BF16)
    kt_b = jnp.where(dim < half, 0.0, kt).astype(BF16)
    for kb in range(nb):
        kt_ref[kb, :, 0:KEY_BLOCK] = kt_a[:, kb * KEY_BLOCK:(kb + 1) * KEY_BLOCK]
        kt_ref[kb, :, KEY_BLOCK:] = kt_b[:, kb * KEY_BLOCK:(kb + 1) * KEY_BLOCK]
    lane = lax.broadcasted_iota(jnp.int32, (seq, LANES), 1)
    v = v_ref[...]
    zero = jnp.zeros_like(v)
    vcat_ref[:, 0:KEY_BLOCK, :] = jnp.where(lane < half, v, zero).reshape(nb, KEY_BLOCK, LANES)
    vcat_ref[:, KEY_BLOCK:, :] = jnp.where(lane < half, zero, v).reshape(nb, KEY_BLOCK, LANES)

    rj = lax.broadcasted_iota(jnp.int32, (2 * kb_cols, kb_cols), 0)
    cs = lax.broadcasted_iota(jnp.int32, (2 * kb_cols, kb_cols), 1)
    rj = jnp.where(rj >= kb_cols, rj - kb_cols, rj)
    in_suffix = ((rj < KEY_BLOCK) == (cs < KEY_BLOCK)) & (rj >= cs)
    sfx_ref[...] = jnp.where(in_suffix, -1.0, 0.0).astype(BF16)
    out_is_a = lax.broadcasted_iota(jnp.int32, (ROW_PART, LANES), 1) < half

    row = lax.broadcasted_iota(jnp.int32, (KEY_BLOCK, kb_cols), 0)
    col = lax.broadcasted_iota(jnp.int32, (KEY_BLOCK, kb_cols), 1)
    on_diagonal = jnp.where(col >= KEY_BLOCK, col - KEY_BLOCK, col) < row
    parts = [slice(r, r + ROW_PART) for r in range(0, Q_TILE, ROW_PART)]

    def mask_diagonal(val):
        top = jnp.where(on_diagonal, val[0:KEY_BLOCK], 0.0)
        return top if val.shape[0] == KEY_BLOCK else jnp.concatenate([top, val[KEY_BLOCK:]], axis=0)

    def visit(jobs):
        flat = [(qb, slot, blk) for qb, slot, blocks in jobs for blk in blocks]
        zs = [_dot(qb[r0:r0 + nr], kt_ref[kb]) for qb, _, (kb, r0, nr, _) in flat]
        sps = []
        for z, (_, _, blk) in zip(zs, flat):
            sp = jnp.maximum(z, 0.0) + jnp.log(1.0 + jnp.exp(-jnp.abs(z)))
            sps.append(mask_diagonal(sp) if blk[3] else sp)
        his = [sp.astype(BF16) for sp in sps]
        los = [(sp - hi.astype(F32)).astype(BF16) for sp, hi in zip(sps, his)]
        sums = [_dot(jnp.concatenate([hi, lo], axis=1), sfx_ref[...]) for hi, lo in zip(his, los)]
        pvs = []
        for s, z, (_, _, (kb, _, _, diagonal)) in zip(sums, zs, flat):
            w = jnp.exp(z + s)
            if diagonal:
                w = mask_diagonal(w)
            pvs.append(_dot(w.astype(BF16), vcat_ref[kb]))
        for _, slot, _ in jobs:
            acc = [acc_ref[slot, part, :] for part in parts]
            carry = [carry_ref[slot, part, :] for part in parts]
            for pv, sp, (_, blk_slot, (_, r0, nr, _)) in zip(pvs, sps, flat):
                if blk_slot != slot:
                    continue
                tot_a = jnp.sum(sp[:, 0:KEY_BLOCK], axis=1, keepdims=True)
                tot_b = jnp.sum(sp[:, KEY_BLOCK:], axis=1, keepdims=True)
                for n, part in enumerate(parts):
                    if r0 <= part.start and part.stop <= r0 + nr:
                        mine = slice(part.start - r0, part.stop - r0)
                        acc[n] = acc[n] + jnp.exp(carry[n]) * pv[mine]
                        carry[n] = carry[n] - jnp.where(out_is_a, tot_a[mine], tot_b[mine])
            for n, part in enumerate(parts):
                acc_ref[slot, part, :] = acc[n]
                carry_ref[slot, part, :] = carry[n]

    def first_blocks(i, with_previous):
        kd = i * per_tile
        blocks = [(kd + 1, KEY_BLOCK, KEY_BLOCK, True), (kd, 0, Q_TILE, True)]
        if with_previous:
            blocks += [(kd - 1, 0, KEY_BLOCK + ROW_PART, False), (kd - 2, 0, ROW_PART, False)]
        return blocks

    def walk_on(i, qb, slot):
        def max_carry():
            return jnp.max(carry_ref[slot])

        kd = i * per_tile

        @pl.when((i > 0) & (max_carry() > EXP_UNDERFLOW_LOG))
        def _():
            visit([(qb, slot, [(kd - 1, KEY_BLOCK + ROW_PART, KEY_BLOCK - ROW_PART, False),
                               (kd - 2, ROW_PART, KEY_BLOCK - ROW_PART, False),
                               (kd - 2, KEY_BLOCK, KEY_BLOCK, False)])])

        left = jnp.maximum(i - 1, 0) * per_tile
        trips = left // (2 * per_tile)

        def k_cond(state):
            n, top = state
            return (n < trips) & (top > EXP_UNDERFLOW_LOG)

        def k_body(state):
            n, _ = state
            first = left - 1 - n * 2 * per_tile
            visit([(qb, slot, [(first - d, 0, Q_TILE, False) for d in range(2 * per_tile)])])
            return n + 1, max_carry()

        n_done, top = lax.while_loop(k_cond, k_body, (0, max_carry()))

        @pl.when((n_done == trips) & (left - trips * 2 * per_tile > 0) & (top > EXP_UNDERFLOW_LOG))
        def _():
            visit([(qb, slot, [(per_tile - 1 - d, 0, Q_TILE, False) for d in range(per_tile)])])

    def pair_body(j, _):
        tiles = (2 * j, 2 * j + 1)
        starts = [pl.multiple_of(i * Q_TILE, Q_TILE) for i in tiles]
        qbs = [q_ref[pl.ds(q0, Q_TILE), :] for q0 in starts]
        acc_ref[...] = jnp.zeros_like(acc_ref)
        carry_ref[...] = jnp.zeros_like(carry_ref)

        @pl.when(j == 0)
        def _():
            visit([(qbs[0], 0, first_blocks(tiles[0], False)), (qbs[1], 1, first_blocks(tiles[1], True))])

        @pl.when(j > 0)
        def _():
            visit([(qb, slot, first_blocks(i, True)) for slot, (i, qb) in enumerate(zip(tiles, qbs))])

        live = jnp.maximum(jnp.where(j > 0, carry_ref[0], carry_ref[1]), carry_ref[1])

        @pl.when(jnp.max(live) > EXP_UNDERFLOW_LOG)
        def _():
            for slot, (i, qb) in enumerate(zip(tiles, qbs)):
                walk_on(i, qb, slot)

        for slot, q0 in enumerate(starts):
            o_ref[pl.ds(q0, Q_TILE), :] = acc_ref[slot].astype(o_ref.dtype)
        return 0

    lax.fori_loop(0, seq // (2 * Q_TILE), pair_body, 0)


def _attention(q, k, v):
    b, s, _ = q.shape
    nb = s // KEY_BLOCK
    spec = pl.BlockSpec((None, s, LANES), lambda bi, j: (bi, 0, j))
    return pl.pallas_call(
        functools.partial(_attn_kernel, seq=s),
        grid=(b, SB_WIDTH // LANES),
        in_specs=[spec, spec, spec],
        out_specs=spec,
        out_shape=jax.ShapeDtypeStruct((b, s, SB_WIDTH), BF16),
        scratch_shapes=[
            pltpu.VMEM((nb, LANES, 2 * KEY_BLOCK), BF16),
            pltpu.VMEM((nb, 2 * KEY_BLOCK, LANES), BF16),
            pltpu.VMEM((4 * KEY_BLOCK, 2 * KEY_BLOCK), BF16),
            pltpu.VMEM((2, Q_TILE, LANES), F32),
            pltpu.VMEM((2, Q_TILE, LANES), F32),
        ],
        compiler_params=_params(2),
        name="stickbreak_attn",
    )(q, k, v)


def _mix0_kernel(a_ref, u_ref, uh_ref, x_ref, pw_ref, ps_ref, wo_ref, g_ref, b_ref, o_ref,
                 ubuf, s2, s4, s8, s16_ref):
    i = pl.program_id(1)
    ts = u_ref.shape[0]
    n = HALO + ts
    gd = POOL_GROUP_DIM
    ubuf[0:HALO, :] = jnp.where(i > 0, uh_ref[...], 0.0)
    ubuf[HALO:, :] = u_ref[...]
    s2[8:n, :] = ubuf[8:n, :] + ubuf[7:n - 1, :]
    s4[16:n, gd:] = s2[16:n, gd:] + s2[14:n - 2, gd:]
    s8[24:n, 2 * gd:] = s4[24:n, 2 * gd:] + s4[20:n - 4, 2 * gd:]
    s16_ref[HALO:n, 3 * gd:] = s8[HALO:n, 3 * gd:] + s8[HALO - 8:n - 8, 3 * gd:]
    window_sums = (s2, s4, s8, s16_ref)

    def project(s, _):
        rows = slice(s * SUB_ROWS, (s + 1) * SUB_ROWS)
        held = slice(HALO + s * SUB_ROWS, HALO + (s + 1) * SUB_ROWS)
        pos1 = (i * ts + s * SUB_ROWS + 1
                + lax.broadcasted_iota(jnp.int32, (SUB_ROWS, gd), 0)).astype(F32)
        mixed = []
        for g, w in enumerate(POOL_WINDOWS):
            cols = slice(g * gd, (g + 1) * gd)
            pooled = window_sums[g][held, cols] / jnp.minimum(pos1, float(w)) - u_ref[rows, cols]
            mixed.append((_dot(pooled.astype(BF16), pw_ref[g]) * ps_ref[:, cols]).astype(BF16))
        cat = jnp.concatenate([a_ref[rows, :]] + mixed, axis=1)
        return _dot(cat, wo_ref[...])

    def normalize(s, mix):
        rows = slice(s * SUB_ROWS, (s + 1) * SUB_ROWS)
        o_ref[rows, :] = _layer_norm(ALPHA * x_ref[rows, :] + mix, g_ref[...], b_ref[...])

    _skewed(ts // SUB_ROWS, [project, normalize])


def _mix0(a, u, x, pool_w, pool_scale, w_out, ln_g, ln_b):
    b, s, _ = x.shape
    ts = ROW_TILE
    tile = lambda bi, i: (bi, i, 0)
    halo = lambda bi, i: (bi, jnp.maximum(i * (ts // HALO) - 1, 0), 0)
    return pl.pallas_call(
        _mix0_kernel,
        grid=(b, s // ts),
        in_specs=[
            pl.BlockSpec((None, ts, SB_WIDTH), tile),
            pl.BlockSpec((None, ts, POOL_WIDTH), tile),
            pl.BlockSpec((None, HALO, POOL_WIDTH), halo),
            pl.BlockSpec((None, ts, D_MODEL), tile),
            _const_spec(pool_w.shape), _const_spec(pool_scale.shape), _const_spec(w_out.shape),
            _const_spec(ln_g.shape), _const_spec(ln_b.shape),
        ],
        out_specs=pl.BlockSpec((None, ts, D_MODEL), tile),
        out_shape=jax.ShapeDtypeStruct((b, s, D_MODEL), F32),
        scratch_shapes=[pltpu.VMEM((HALO + ts, POOL_WIDTH), F32)] * 5,
        compiler_params=_params(2),
        name="mix0",
    )(a, u, u, x, pool_w, pool_scale, w_out, ln_g, ln_b)


def _ffn_kernel(x_ref, p_ref, wg_ref, wu_ref, wd_ref, g_ref, b_ref, wpg_ref, bpg_ref, wpp_ref, o_ref,
                h_ref):
    def rows(s):
        return slice(s * SUB_ROWS, (s + 1) * SUB_ROWS)

    def swiglu(s, _):
        xb = x_ref[rows(s), :].astype(BF16)
        for c in range(D_FF // FF_CHUNK):
            cols = slice(c * FF_CHUNK, (c + 1) * FF_CHUNK)
            gate = _dot(xb, wg_ref[:, cols])
            up = _dot(xb, wu_ref[:, cols])
            h_ref[rows(s), cols] = (jax.nn.silu(gate) * up).astype(BF16)
        return _dot(h_ref[rows(s), :], wd_ref[...])

    def norm_and_embed(s, y):
        x2 = _layer_norm(ALPHA * x_ref[rows(s), :] + y, g_ref[...], b_ref[...])
        ple_gate = jax.nn.sigmoid(_dot(x2.astype(BF16), wpg_ref[...]) + bpg_ref[...])
        o_ref[rows(s), :] = x2 + ple_gate * _dot(p_ref[rows(s), :].astype(BF16), wpp_ref[...])

    _skewed(x_ref.shape[0] // SUB_ROWS, [swiglu, norm_and_embed])


def _ffn(x2d, p3d, layer, w_gate, w_up, w_down, ln_g, ln_b, w_pg, b_pg, w_pp):
    t = x2d.shape[0]
    row = lambda i: (i, 0)
    consts = (w_gate, w_up, w_down, ln_g, ln_b, w_pg, b_pg, w_pp)

    return pl.pallas_call(
        _ffn_kernel,
        grid=(t // ROW_TILE,),
        in_specs=[pl.BlockSpec((ROW_TILE, D_MODEL), row),
                  pl.BlockSpec((None, ROW_TILE, PLE_DIM), lambda i: (layer, i, 0))]
        + [_layer_spec(c, layer) for c in consts],
        out_specs=pl.BlockSpec((ROW_TILE, D_MODEL), row),
        out_shape=jax.ShapeDtypeStruct((t, D_MODEL), F32),
        scratch_shapes=[pltpu.VMEM((ROW_TILE, D_FF), BF16)],
        compiler_params=_params(1),
        name="ffn_ple",
    )(x2d, p3d, *consts)


def _merge(primary, secondary):
    merged, j = [], 0
    for k, item in enumerate(primary):
        merged.append(item)
        while j < len(secondary) and (j + 1) * len(primary) <= (k + 1) * len(secondary):
            merged.append(secondary[j])
            j += 1
    return merged + secondary[j:]


def _channel_mixer_items(x1_ref, slot, p_ref, o_ref, h_ref, wg_ref, wu_ref, wd_ref, g_ref, b_ref, wpg_ref,
                         bpg_ref, wpp_ref):
    held = {}

    def load_x1():
        held["x1b"] = x1_ref[slot].astype(BF16)

    def swiglu_chunk(k):
        def run():
            cols = slice(k * FF_CHUNK, (k + 1) * FF_CHUNK)
            gate = _dot(held["x1b"], wg_ref[:, cols])
            up = _dot(held["x1b"], wu_ref[:, cols])
            h_ref[:, cols] = (jax.nn.silu(gate) * up).astype(BF16)
        return run

    def project_down():
        held["y"] = _dot(h_ref[...], wd_ref[...])

    def norm_and_embed():
        x2 = _layer_norm(ALPHA * x1_ref[slot] + held.pop("y"), g_ref[...], b_ref[...])
        ple_gate = jax.nn.sigmoid(_dot(x2.astype(BF16), wpg_ref[...]) + bpg_ref[...])
        o_ref[...] = x2 + ple_gate * _dot(p_ref[...].astype(BF16), wpp_ref[...])

    return [load_x1] + [swiglu_chunk(k) for k in range(D_FF // FF_CHUNK)] + [project_down, norm_and_embed]


def _layer1_kernel(x_ref, p_ref, wi_ref, sg_g_ref, sg_b_ref, dw_ref, cg_ref, cb_ref, sw_ref, sb_ref, wo_ref,
                   g1_ref, b1_ref, wg_ref, wu_ref, wd_ref, g2_ref, b2_ref, wpg_ref, bpg_ref, wpp_ref,
                   o_ref, hc_ref, shift_ref, conv_ref, cat_ref, u_ref, v_ref, x1_ref, h_ref, *, tiles_per_seq):
    step = pl.program_id(0)
    ts = x_ref.shape[0]
    c = CONV_WIDTH
    span = ts + HALO - SUBLANES

    @pl.when(step % tiles_per_seq == 0)
    def _():
        hc_ref[0:HALO, :] = jnp.zeros((HALO, c), F32)

    @pl.when(step == 0)
    def _():
        x1_ref[1] = jnp.zeros(x1_ref.shape[1:], F32)

    fill = step % 2
    drain = 1 - fill
    held = {}

    def project_in():
        xb = x_ref[...].astype(BF16)
        held["in"] = [_dot(xb, wi_ref[:, lo:lo + c]) for lo in (0, c, 2 * c, 2 * c + SG_WIDTH)]

    def activate():
        a, gl, zu, zv = held.pop("in")
        hc_ref[HALO:HALO + ts, :] = a * jax.nn.sigmoid(gl)
        u_ref[...] = jax.nn.gelu(zu)
        v_ref[...] = _layer_norm(jax.nn.gelu(zv), sg_g_ref[...], sg_b_ref[...]).astype(BF16)

    def shift_copies():
        for r in range(1, SUBLANES):
            shift_ref[r - 1, 0:span, :] = hc_ref[r:r + span, :]

    first = HALO - (CONV_TAPS - 1)

    def conv_piece(lb, rb):
        def run():
            cols = slice(lb * LANES, (lb + 1) * LANES)
            acc = jnp.zeros((CONV_ROWS, LANES), F32)
            for kk in range(CONV_TAPS):
                r, r0 = (first + kk) % SUBLANES, rb * CONV_ROWS + (first + kk) // SUBLANES * SUBLANES
                taps = (hc_ref[r0:r0 + CONV_ROWS, cols] if r == 0 else shift_ref[r - 1, r0:r0 + CONV_ROWS, cols])
                acc = acc + dw_ref[kk:kk + 1, cols] * taps
            conv_ref[rb * CONV_ROWS:(rb + 1) * CONV_ROWS, cols] = acc
        return run

    def conv_norm():
        cat_ref[:, 0:c] = jax.nn.silu(_layer_norm(conv_ref[...], cg_ref[...], cb_ref[...])).astype(BF16)

    tt = lax.broadcasted_iota(jnp.int32, (SG_CHUNK, SG_CHUNK), 0)
    ss = lax.broadcasted_iota(jnp.int32, (SG_CHUNK, SG_CHUNK), 1)

    def gating(g):
        def run():
            cols = slice(g * SG_GROUP_DIM, (g + 1) * SG_GROUP_DIM)
            chunks = [slice(k * SG_CHUNK, (k + 1) * SG_CHUNK) for k in range(ts // SG_CHUNK)]
            gate_w = jnp.where(ss <= tt, sw_ref[g], 0.0).astype(BF16)
            sv = _dot(gate_w, jnp.concatenate([v_ref[ch, cols] for ch in chunks], axis=1))
            for k, ch in enumerate(chunks):
                gated = u_ref[ch, cols] * (sv[:, k * SG_GROUP_DIM:(k + 1) * SG_GROUP_DIM] + sb_ref[:, cols])
                cat_ref[ch, c + g * SG_GROUP_DIM:c + (g + 1) * SG_GROUP_DIM] = gated.astype(BF16)
        return run

    def project_out():
        mix = _dot(cat_ref[...], wo_ref[...])
        x1_ref[fill] = _layer_norm(ALPHA * x_ref[...] + mix, g1_ref[...], b1_ref[...])
        hc_ref[0:HALO, :] = hc_ref[ts:ts + HALO, :]

    token_items = ([project_in, activate, shift_copies]
                   + [conv_piece(lb, rb) for lb in range(c // LANES) for rb in range(ts // CONV_ROWS)]
                   + [conv_norm] + [gating(g) for g in range(SG_GROUPS)] + [project_out])

    channel_items = _channel_mixer_items(x1_ref, drain, p_ref, o_ref, h_ref, wg_ref, wu_ref, wd_ref, g2_ref,
                                         b2_ref, wpg_ref, bpg_ref, wpp_ref)
    for item in token_items[:3] + _merge(channel_items, token_items[3:]):
        item()


def _layer1(x2d, p3d, layer, seq, w_in, sg_ln_g, sg_ln_b, conv_dw, conv_g, conv_b, sg_w, sg_bias, w_out,
            ln_g, ln_b, ffn_params):
    t = x2d.shape[0]
    ts = FUSED_TILE
    n = t // ts
    mixer_consts = (w_in, sg_ln_g, sg_ln_b, conv_dw, conv_g, conv_b, sg_w, sg_bias, w_out, ln_g, ln_b)
    previous = lambda g: jnp.maximum(g - 1, 0)
    return pl.pallas_call(
        functools.partial(_layer1_kernel, tiles_per_seq=seq // ts),
        grid=(n + 1,),
        in_specs=[pl.BlockSpec((ts, D_MODEL), lambda g: (jnp.minimum(g, n - 1), 0)),
                  pl.BlockSpec((None, ts, PLE_DIM), lambda g: (layer, previous(g), 0))]
        + [_const_spec(a.shape) for a in mixer_consts] + [_layer_spec(a, layer) for a in ffn_params],
        out_specs=pl.BlockSpec((ts, D_MODEL), lambda g: (previous(g), 0)),
        out_shape=jax.ShapeDtypeStruct((t, D_MODEL), F32),
        scratch_shapes=[
            pltpu.VMEM((HALO + ts, CONV_WIDTH), F32),
            pltpu.VMEM((SUBLANES - 1, ts + HALO - SUBLANES, CONV_WIDTH), F32),
            pltpu.VMEM((ts, CONV_WIDTH), F32),
            pltpu.VMEM((ts, MIX_WIDTH), BF16),
            pltpu.VMEM((ts, SG_WIDTH), F32),
            pltpu.VMEM((ts, SG_WIDTH), BF16),
            pltpu.VMEM((2, ts, D_MODEL), F32),
            pltpu.VMEM((ts, D_FF), BF16),
        ],
        compiler_params=_params(1),
        name="layer1",
    )(x2d, p3d, *mixer_consts, *ffn_params)


def _row(vec):
    return vec.reshape(1, -1)


def kernel(x, p, even_w_in, even_w_out, pool_w, pool_scale, odd_w_in, odd_w_out, conv_dw, conv_ln_g,
           conv_ln_b, sg_ln_g, sg_ln_b, sg_w, sg_b, ln_mix_g, ln_mix_b, ffn_w_gate, ffn_w_up, ffn_w_down,
           ln_ffn_g, ln_ffn_b, ple_w_proj, ple_w_gate, ple_b_gate):
    b, s, d = x.shape
    t = b * s
    bf = lambda w: w.astype(BF16)

    p3d = p.reshape(DEPTH, t, PLE_DIM)
    rows = lambda m: m.reshape(DEPTH, 1, -1)
    ffn_params = (bf(ffn_w_gate), bf(ffn_w_up), bf(ffn_w_down), rows(ln_ffn_g), rows(ln_ffn_b),
                  bf(ple_w_gate), rows(ple_b_gate), bf(ple_w_proj))

    q, k, v, u = _inproj0(x.reshape(t, d), bf(even_w_in[0]))
    attn = _attention(q.reshape(b, s, SB_WIDTH), k.reshape(b, s, SB_WIDTH), v.reshape(b, s, SB_WIDTH))
    x = _mix0(attn, u.reshape(b, s, POOL_WIDTH), x, bf(pool_w[0]), _row(pool_scale[0]), bf(even_w_out[0]),
              _row(ln_mix_g[0]), _row(ln_mix_b[0]))
    x = _ffn(x.reshape(t, d), p3d, 0, *ffn_params)

    sg_bias = jnp.repeat(sg_b[0].T, SG_GROUP_DIM, axis=1)
    return _layer1(x, p3d, 1, s, bf(odd_w_in[0]), _row(sg_ln_g[0]), _row(sg_ln_b[0]), conv_dw[0],
                   _row(conv_ln_g[0]), _row(conv_ln_b[0]), sg_w[0], sg_bias, bf(odd_w_out[0]), _row(ln_mix_g[1]),
                   _row(ln_mix_b[1]), ffn_params).reshape(b, s, d)
```

```python
import functools

import jax
import jax.numpy as jnp
from jax import lax
from jax.experimental import pallas as pl
from jax.experimental.pallas import tpu as pltpu

F32 = jnp.float32
BF16 = jnp.bfloat16

D_MODEL = 1024
N_SB_HEADS = 8
SB_HEAD_DIM = 64
SB_WIDTH = N_SB_HEADS * SB_HEAD_DIM
POOL_WINDOWS = (2, 4, 8, 16)
POOL_GROUP_DIM = 128
POOL_WIDTH = len(POOL_WINDOWS) * POOL_GROUP_DIM
CONV_WIDTH = 512
CONV_TAPS = 31
SG_GROUPS = 4
SG_GROUP_DIM = 128
SG_WIDTH = SG_GROUPS * SG_GROUP_DIM
SG_CHUNK = 128
MIX_WIDTH = 1024
D_FF = 2816
PLE_DIM = 256
DEPTH = 2
ALPHA = (2 * DEPTH) ** 0.25
LN_EPS = 1e-5

LANES = 128
SUBLANES = 8
MXU_DIM = 256
VMEM_LIMIT = 56 * 1024 * 1024

ROW_TILE = 1024
SUB_ROWS = 256
FUSED_TILE = 512
HALO = 32
KEY_BLOCK = 128
Q_TILE = 256
ROW_PART = 64
EXP_UNDERFLOW_LOG = -104.0
FF_CHUNK = 256
CONV_ROWS = 64


def _layer_norm(y, g, b):
    mu = jnp.mean(y, axis=-1, keepdims=True)
    d = y - mu
    var = jnp.mean(d * d, axis=-1, keepdims=True)
    return d * lax.rsqrt(var + LN_EPS) * g + b


def _dot(a, b):
    return jnp.dot(a, b, preferred_element_type=F32)


def _const_spec(shape):
    zeros = (0,) * len(shape)
    return pl.BlockSpec(shape, lambda *_: zeros, pipeline_mode=pl.Buffered(1))


def _layer_spec(arr, layer):
    index = (layer,) + (0,) * (arr.ndim - 1)
    return pl.BlockSpec((None,) + arr.shape[1:], lambda *_: index, pipeline_mode=pl.Buffered(1))


def _params(n_axes):
    return pltpu.CompilerParams(dimension_semantics=("arbitrary",) * n_axes,
                                vmem_limit_bytes=VMEM_LIMIT)


def _skewed(n_sub, stages):
    values = [None] * n_sub
    for step in range(n_sub + len(stages) - 1):
        for k, stage in enumerate(stages):
            s = step - k
            if 0 <= s < n_sub:
                values[s] = stage(s, values[s])


def _inproj0_kernel(x_ref, w_ref, q_ref, k_ref, v_ref, u_ref):
    w = SB_WIDTH
    for s in range(x_ref.shape[0] // SUB_ROWS):
        rows = slice(s * SUB_ROWS, (s + 1) * SUB_ROWS)
        xb = x_ref[rows, :].astype(BF16)
        q_ref[rows, :] = (_dot(xb, w_ref[:, 0:w]) * (SB_HEAD_DIM ** -0.5)).astype(BF16)
        k_ref[rows, :] = _dot(xb, w_ref[:, w:2 * w]).astype(BF16)
        v_ref[rows, :] = _dot(xb, w_ref[:, 2 * w:3 * w]).astype(BF16)
        u_ref[rows, :] = _dot(xb, w_ref[:, 3 * w:])


def _inproj0(x2d, w_in):
    t = x2d.shape[0]
    row = lambda i: (i, 0)
    return pl.pallas_call(
        _inproj0_kernel,
        grid=(t // ROW_TILE,),
        in_specs=[pl.BlockSpec((ROW_TILE, D_MODEL), row), _const_spec(w_in.shape)],
        out_specs=[pl.BlockSpec((ROW_TILE, SB_WIDTH), row)] * 3 + [pl.BlockSpec((ROW_TILE, POOL_WIDTH), row)],
        out_shape=[jax.ShapeDtypeStruct((t, SB_WIDTH), BF16)] * 3 + [jax.ShapeDtypeStruct((t, POOL_WIDTH), F32)],
        compiler_params=_params(1),
        name="inproj0",
    )(x2d, w_in)


def _attn_kernel(q_ref, k_ref, v_ref, o_ref, kt_ref, vcat_ref, sfx_ref, acc_ref, carry_ref, *, seq):
    nb = seq // KEY_BLOCK
    kb_cols = 2 * KEY_BLOCK
    half = LANES // 2
    per_tile = Q_TILE // KEY_BLOCK

    kt = k_ref[...].astype(F32).T
    dim = lax.broadcasted_iota(jnp.int32, kt.shape, 0)
    kt_a = jnp.where(dim < half, kt, 0.0).astype(BF16)
    kt_b = jnp.where(dim < half, 0.0, kt).astype(BF16)
    for kb in range(nb):
        kt_ref[kb, :, 0:KEY_BLOCK] = kt_a[:, kb * KEY_BLOCK:(kb + 1) * KEY_BLOCK]
        kt_ref[kb, :, KEY_BLOCK:] = kt_b[:, kb * KEY_BLOCK:(kb + 1) * KEY_BLOCK]
    lane = lax.broadcasted_iota(jnp.int32, (seq, LANES), 1)
    v = v_ref[...]
    zero = jnp.zeros_like(v)
    vcat_ref[:, 0:KEY_BLOCK, :] = jnp.where(lane < half, v, zero).reshape(nb, KEY_BLOCK, LANES)
    vcat_ref[:, KEY_BLOCK:, :] = jnp.where(lane < half, zero, v).reshape(nb, KEY_BLOCK, LANES)

    rj = lax.broadcasted_iota(jnp.int32, (kb_cols, kb_cols), 0)
    cs = lax.broadcasted_iota(jnp.int32, (kb_cols, kb_cols), 1)
    in_suffix = ((rj < KEY_BLOCK) == (cs < KEY_BLOCK)) & (rj >= cs)
    sfx_ref[...] = jnp.where(in_suffix, -1.0, 0.0).astype(BF16)
    out_is_a = lax.broadcasted_iota(jnp.int32, (ROW_PART, LANES), 1) < half

    row = lax.broadcasted_iota(jnp.int32, (KEY_BLOCK, kb_cols), 0)
    col = lax.broadcasted_iota(jnp.int32, (KEY_BLOCK, kb_cols), 1)
    on_diagonal = jnp.where(col >= KEY_BLOCK, col - KEY_BLOCK, col) < row
    parts = [slice(r, r + ROW_PART) for r in range(0, Q_TILE, ROW_PART)]

    def mask_diagonal(val):
        top = jnp.where(on_diagonal, val[0:KEY_BLOCK], 0.0)
        return top if val.shape[0] == KEY_BLOCK else jnp.concatenate([top, val[KEY_BLOCK:]], axis=0)

    def visit(jobs):
        flat = [(qb, slot, blk) for qb, slot, blocks in jobs for blk in blocks]
        zs = [_dot(qb[r0:r0 + nr], kt_ref[kb]) for qb, _, (kb, r0, nr, _) in flat]
        sps = []
        for z, (_, _, blk) in zip(zs, flat):
            sp = jnp.maximum(z, 0.0) + jnp.log(1.0 + jnp.exp(-jnp.abs(z)))
            sps.append(mask_diagonal(sp) if blk[3] else sp)
        sums = [_dot(sp.astype(BF16), sfx_ref[...]) for sp in sps]
        pvs = []
        for s, z, (_, _, (kb, _, _, diagonal)) in zip(sums, zs, flat):
            w = jnp.exp(z + s)
            if diagonal:
                w = mask_diagonal(w)
            pvs.append(_dot(w.astype(BF16), vcat_ref[kb]))
        for _, slot, _ in jobs:
            acc = [acc_ref[slot, part, :] for part in parts]
            carry = [carry_ref[slot, part, :] for part in parts]
            for pv, sp, (_, blk_slot, (_, r0, nr, _)) in zip(pvs, sps, flat):
                if blk_slot != slot:
                    continue
                tot_a = jnp.sum(sp[:, 0:KEY_BLOCK], axis=1, keepdims=True)
                tot_b = jnp.sum(sp[:, KEY_BLOCK:], axis=1, keepdims=True)
                for n, part in enumerate(parts):
                    if r0 <= part.start and part.stop <= r0 + nr:
                        mine = slice(part.start - r0, part.stop - r0)
                        acc[n] = acc[n] + jnp.exp(carry[n]) * pv[mine]
                        carry[n] = carry[n] - jnp.where(out_is_a, tot_a[mine], tot_b[mine])
            for n, part in enumerate(parts):
                acc_ref[slot, part, :] = acc[n]
                carry_ref[slot, part, :] = carry[n]

    def first_blocks(i, with_previous):
        kd = i * per_tile
        blocks = [(kd + 1, KEY_BLOCK, KEY_BLOCK, True), (kd, 0, Q_TILE, True)]
        if with_previous:
            blocks += [(kd - 1, 0, KEY_BLOCK + ROW_PART, False), (kd - 2, 0, ROW_PART, False)]
        return blocks

    def walk_on(i, qb, slot):
        def max_carry():
            return jnp.max(carry_ref[slot])

        kd = i * per_tile

        @pl.when((i > 0) & (max_carry() > EXP_UNDERFLOW_LOG))
        def _():
            visit([(qb, slot, [(kd - 1, KEY_BLOCK + ROW_PART, KEY_BLOCK - ROW_PART, False),
                               (kd - 2, ROW_PART, KEY_BLOCK - ROW_PART, False),
                               (kd - 2, KEY_BLOCK, KEY_BLOCK, False)])])

        left = jnp.maximum(i - 1, 0) * per_tile
        trips = left // (2 * per_tile)

        def k_cond(state):
            n, top = state
            return (n < trips) & (top > EXP_UNDERFLOW_LOG)

        def k_body(state):
            n, _ = state
            first = left - 1 - n * 2 * per_tile
            visit([(qb, slot, [(first - d, 0, Q_TILE, False) for d in range(2 * per_tile)])])
            return n + 1, max_carry()

        n_done, top = lax.while_loop(k_cond, k_body, (0, max_carry()))

        @pl.when((n_done == trips) & (left - trips * 2 * per_tile > 0) & (top > EXP_UNDERFLOW_LOG))
        def _():
            visit([(qb, slot, [(per_tile - 1 - d, 0, Q_TILE, False) for d in range(per_tile)])])

    def pair_body(j, _):
        tiles = (2 * j, 2 * j + 1)
        starts = [pl.multiple_of(i * Q_TILE, Q_TILE) for i in tiles]
        qbs = [q_ref[pl.ds(q0, Q_TILE), :] for q0 in starts]
        acc_ref[...] = jnp.zeros_like(acc_ref)
        carry_ref[...] = jnp.zeros_like(carry_ref)

        @pl.when(j == 0)
        def _():
            visit([(qbs[0], 0, first_blocks(tiles[0], False)), (qbs[1], 1, first_blocks(tiles[1], True))])

        @pl.when(j > 0)
        def _():
            visit([(qb, slot, first_blocks(i, True)) for slot, (i, qb) in enumerate(zip(tiles, qbs))])

        live = jnp.maximum(jnp.where(j > 0, carry_ref[0], carry_ref[1]), carry_ref[1])

        @pl.when(jnp.max(live) > EXP_UNDERFLOW_LOG)
        def _():
            for slot, (i, qb) in enumerate(zip(tiles, qbs)):
                walk_on(i, qb, slot)

        for slot, q0 in enumerate(starts):
            o_ref[pl.ds(q0, Q_TILE), :] = acc_ref[slot].astype(o_ref.dtype)
        return 0

    lax.fori_loop(0, seq // (2 * Q_TILE), pair_body, 0)


def _attention(q, k, v):
    b, s, _ = q.shape
    nb = s // KEY_BLOCK
    spec = pl.BlockSpec((None, s, LANES), lambda bi, j: (bi, 0, j))
    return pl.pallas_call(
        functools.partial(_attn_kernel, seq=s),
        grid=(b, SB_WIDTH // LANES),
        in_specs=[spec, spec, spec],
        out_specs=spec,
        out_shape=jax.ShapeDtypeStruct((b, s, SB_WIDTH), BF16),
        scratch_shapes=[
            pltpu.VMEM((nb, LANES, 2 * KEY_BLOCK), BF16),
            pltpu.VMEM((nb, 2 * KEY_BLOCK, LANES), BF16),
            pltpu.VMEM((2 * KEY_BLOCK, 2 * KEY_BLOCK), BF16),
            pltpu.VMEM((2, Q_TILE, LANES), F32),
            pltpu.VMEM((2, Q_TILE, LANES), F32),
        ],
        compiler_params=_params(2),
        name="stickbreak_attn",
    )(q, k, v)


def _mix0_kernel(a_ref, u_ref, uh_ref, x_ref, pw_ref, ps_ref, wo_ref, g_ref, b_ref, o_ref,
                 ubuf, s2, s4, s8, s16_ref):
    i = pl.program_id(1)
    ts = u_ref.shape[0]
    n = HALO + ts
    gd = POOL_GROUP_DIM
    ubuf[0:HALO, :] = jnp.where(i > 0, uh_ref[...], 0.0)
    ubuf[HALO:, :] = u_ref[...]
    s2[8:n, :] = ubuf[8:n, :] + ubuf[7:n - 1, :]
    s4[16:n, gd:] = s2[16:n, gd:] + s2[14:n - 2, gd:]
    s8[24:n, 2 * gd:] = s4[24:n, 2 * gd:] + s4[20:n - 4, 2 * gd:]
    s16_ref[HALO:n, 3 * gd:] = s8[HALO:n, 3 * gd:] + s8[HALO - 8:n - 8, 3 * gd:]
    window_sums = (s2, s4, s8, s16_ref)

    def project(s, _):
        rows = slice(s * SUB_ROWS, (s + 1) * SUB_ROWS)
        held = slice(HALO + s * SUB_ROWS, HALO + (s + 1) * SUB_ROWS)
        pos1 = (i * ts + s * SUB_ROWS + 1
                + lax.broadcasted_iota(jnp.int32, (SUB_ROWS, gd), 0)).astype(F32)
        mixed = []
        for g, w in enumerate(POOL_WINDOWS):
            cols = slice(g * gd, (g + 1) * gd)
            pooled = window_sums[g][held, cols] / jnp.minimum(pos1, float(w)) - u_ref[rows, cols]
            mixed.append((_dot(pooled.astype(BF16), pw_ref[g]) * ps_ref[:, cols]).astype(BF16))
        cat = jnp.concatenate([a_ref[rows, :]] + mixed, axis=1)
        return _dot(cat, wo_ref[...])

    def normalize(s, mix):
        rows = slice(s * SUB_ROWS, (s + 1) * SUB_ROWS)
        o_ref[rows, :] = _layer_norm(ALPHA * x_ref[rows, :] + mix, g_ref[...], b_ref[...])

    _skewed(ts // SUB_ROWS, [project, normalize])


def _mix0(a, u, x, pool_w, pool_scale, w_out, ln_g, ln_b):
    b, s, _ = x.shape
    ts = ROW_TILE
    tile = lambda bi, i: (bi, i, 0)
    halo = lambda bi, i: (bi, jnp.maximum(i * (ts // HALO) - 1, 0), 0)
    return pl.pallas_call(
        _mix0_kernel,
        grid=(b, s // ts),
        in_specs=[
            pl.BlockSpec((None, ts, SB_WIDTH), tile),
            pl.BlockSpec((None, ts, POOL_WIDTH), tile),
            pl.BlockSpec((None, HALO, POOL_WIDTH), halo),
            pl.BlockSpec((None, ts, D_MODEL), tile),
            _const_spec(pool_w.shape), _const_spec(pool_scale.shape), _const_spec(w_out.shape),
            _const_spec(ln_g.shape), _const_spec(ln_b.shape),
        ],
        out_specs=pl.BlockSpec((None, ts, D_MODEL), tile),
        out_shape=jax.ShapeDtypeStruct((b, s, D_MODEL), F32),
        scratch_shapes=[pltpu.VMEM((HALO + ts, POOL_WIDTH), F32)] * 5,
        compiler_params=_params(2),
        name="mix0",
    )(a, u, u, x, pool_w, pool_scale, w_out, ln_g, ln_b)


def _ffn_kernel(x_ref, p_ref, wg_ref, wu_ref, wd_ref, g_ref, b_ref, wpg_ref, bpg_ref, wpp_ref, o_ref,
                h_ref):
    def rows(s):
        return slice(s * SUB_ROWS, (s + 1) * SUB_ROWS)

    def swiglu(s, _):
        xb = x_ref[rows(s), :].astype(BF16)
        for c in range(D_FF // FF_CHUNK):
            cols = slice(c * FF_CHUNK, (c + 1) * FF_CHUNK)
            gate = _dot(xb, wg_ref[:, cols])
            up = _dot(xb, wu_ref[:, cols])
            h_ref[rows(s), cols] = (jax.nn.silu(gate) * up).astype(BF16)
        return _dot(h_ref[rows(s), :], wd_ref[...])

    def norm_and_embed(s, y):
        x2 = _layer_norm(ALPHA * x_ref[rows(s), :] + y, g_ref[...], b_ref[...])
        ple_gate = jax.nn.sigmoid(_dot(x2.astype(BF16), wpg_ref[...]) + bpg_ref[...])
        o_ref[rows(s), :] = x2 + ple_gate * _dot(p_ref[rows(s), :].astype(BF16), wpp_ref[...])

    _skewed(x_ref.shape[0] // SUB_ROWS, [swiglu, norm_and_embed])


def _ffn(x2d, p3d, layer, w_gate, w_up, w_down, ln_g, ln_b, w_pg, b_pg, w_pp):
    t = x2d.shape[0]
    row = lambda i: (i, 0)
    consts = (w_gate, w_up, w_down, ln_g, ln_b, w_pg, b_pg, w_pp)

    return pl.pallas_call(
        _ffn_kernel,
        grid=(t // ROW_TILE,),
        in_specs=[pl.BlockSpec((ROW_TILE, D_MODEL), row),
                  pl.BlockSpec((None, ROW_TILE, PLE_DIM), lambda i: (layer, i, 0))]
        + [_layer_spec(c, layer) for c in consts],
        out_specs=pl.BlockSpec((ROW_TILE, D_MODEL), row),
        out_shape=jax.ShapeDtypeStruct((t, D_MODEL), F32),
        scratch_shapes=[pltpu.VMEM((ROW_TILE, D_FF), BF16)],
        compiler_params=_params(1),
        name="ffn_ple",
    )(x2d, p3d, *consts)


def _merge(primary, secondary):
    merged, j = [], 0
    for k, item in enumerate(primary):
        merged.append(item)
        while j < len(secondary) and (j + 1) * len(primary) <= (k + 1) * len(secondary):
            merged.append(secondary[j])
            j += 1
    return merged + secondary[j:]


def _channel_mixer_items(x1_ref, slot, p_ref, o_ref, h_ref, wg_ref, wu_ref, wd_ref, g_ref, b_ref, wpg_ref,
                         bpg_ref, wpp_ref):
    held = {}

    def load_x1():
        held["x1b"] = x1_ref[slot].astype(BF16)

    def swiglu_chunk(k):
        def run():
            cols = slice(k * FF_CHUNK, (k + 1) * FF_CHUNK)
            gate = _dot(held["x1b"], wg_ref[:, cols])
            up = _dot(held["x1b"], wu_ref[:, cols])
            h_ref[:, cols] = (jax.nn.silu(gate) * up).astype(BF16)
        return run

    def project_down():
        held["y"] = _dot(h_ref[...], wd_ref[...])

    def norm_and_embed():
        x2 = _layer_norm(ALPHA * x1_ref[slot] + held.pop("y"), g_ref[...], b_ref[...])
        ple_gate = jax.nn.sigmoid(_dot(x2.astype(BF16), wpg_ref[...]) + bpg_ref[...])
        o_ref[...] = x2 + ple_gate * _dot(p_ref[...].astype(BF16), wpp_ref[...])

    return [load_x1] + [swiglu_chunk(k) for k in range(D_FF // FF_CHUNK)] + [project_down, norm_and_embed]


def _layer1_kernel(x_ref, p_ref, wi_ref, sg_g_ref, sg_b_ref, dw_ref, cg_ref, cb_ref, sw_ref, sb_ref, wo_ref,
                   g1_ref, b1_ref, wg_ref, wu_ref, wd_ref, g2_ref, b2_ref, wpg_ref, bpg_ref, wpp_ref,
                   o_ref, hc_ref, shift_ref, conv_ref, cat_ref, u_ref, v_ref, x1_ref, h_ref, *, tiles_per_seq):
    step = pl.program_id(0)
    ts = x_ref.shape[0]
    c = CONV_WIDTH
    span = ts + HALO - SUBLANES

    @pl.when(step % tiles_per_seq == 0)
    def _():
        hc_ref[0:HALO, :] = jnp.zeros((HALO, c), F32)

    @pl.when(step == 0)
    def _():
        x1_ref[1] = jnp.zeros(x1_ref.shape[1:], F32)

    fill = step % 2
    drain = 1 - fill
    held = {}

    def project_in():
        xb = x_ref[...].astype(BF16)
        held["in"] = [_dot(xb, wi_ref[:, lo:lo + c]) for lo in (0, c, 2 * c, 2 * c + SG_WIDTH)]

    def activate():
        a, gl, zu, zv = held.pop("in")
        hc_ref[HALO:HALO + ts, :] = a * jax.nn.sigmoid(gl)
        u_ref[...] = jax.nn.gelu(zu)
        v_ref[...] = _layer_norm(jax.nn.gelu(zv), sg_g_ref[...], sg_b_ref[...]).astype(BF16)

    def shift_copies():
        for r in range(1, SUBLANES):
            shift_ref[r - 1, 0:span, :] = hc_ref[r:r + span, :]

    first = HALO - (CONV_TAPS - 1)

    def conv_piece(lb, rb):
        def run():
            cols = slice(lb * LANES, (lb + 1) * LANES)
            acc = jnp.zeros((CONV_ROWS, LANES), F32)
            for kk in range(CONV_TAPS):
                r, r0 = (first + kk) % SUBLANES, rb * CONV_ROWS + (first + kk) // SUBLANES * SUBLANES
                taps = (hc_ref[r0:r0 + CONV_ROWS, cols] if r == 0 else shift_ref[r - 1, r0:r0 + CONV_ROWS, cols])
                acc = acc + dw_ref[kk:kk + 1, cols] * taps
            conv_ref[rb * CONV_ROWS:(rb + 1) * CONV_ROWS, cols] = acc
        return run

    def conv_norm():
        cat_ref[:, 0:c] = jax.nn.silu(_layer_norm(conv_ref[...], cg_ref[...], cb_ref[...])).astype(BF16)

    tt = lax.broadcasted_iota(jnp.int32, (SG_CHUNK, SG_CHUNK), 0)
    ss = lax.broadcasted_iota(jnp.int32, (SG_CHUNK, SG_CHUNK), 1)

    def gating(g):
        def run():
            cols = slice(g * SG_GROUP_DIM, (g + 1) * SG_GROUP_DIM)
            chunks = [slice(k * SG_CHUNK, (k + 1) * SG_CHUNK) for k in range(ts // SG_CHUNK)]
            gate_w = jnp.where(ss <= tt, sw_ref[g], 0.0).astype(BF16)
            sv = _dot(gate_w, jnp.concatenate([v_ref[ch, cols] for ch in chunks], axis=1))
            for k, ch in enumerate(chunks):
                gated = u_ref[ch, cols] * (sv[:, k * SG_GROUP_DIM:(k + 1) * SG_GROUP_DIM] + sb_ref[:, cols])
                cat_ref[ch, c + g * SG_GROUP_DIM:c + (g + 1) * SG_GROUP_DIM] = gated.astype(BF16)
        return run

    def project_out():
        mix = _dot(cat_ref[...], wo_ref[...])
        x1_ref[fill] = _layer_norm(ALPHA * x_ref[...] + mix, g1_ref[...], b1_ref[...])
        hc_ref[0:HALO, :] = hc_ref[ts:ts + HALO, :]

    token_items = ([project_in, activate, shift_copies]
                   + [conv_piece(lb, rb) for lb in range(c // LANES) for rb in range(ts // CONV_ROWS)]
                   + [conv_norm] + [gating(g) for g in range(SG_GROUPS)] + [project_out])

    channel_items = _channel_mixer_items(x1_ref, drain, p_ref, o_ref, h_ref, wg_ref, wu_ref, wd_ref, g2_ref,
                                         b2_ref, wpg_ref, bpg_ref, wpp_ref)
    for item in token_items[:3] + _merge(channel_items, token_items[3:]):
        item()


def _layer1(x2d, p3d, layer, seq, w_in, sg_ln_g, sg_ln_b, conv_dw, conv_g, conv_b, sg_w, sg_bias, w_out,
            ln_g, ln_b, ffn_params):
    t = x2d.shape[0]
    ts = FUSED_TILE
    n = t // ts
    mixer_consts = (w_in, sg_ln_g, sg_ln_b, conv_dw, conv_g, conv_b, sg_w, sg_bias, w_out, ln_g, ln_b)
    previous = lambda g: jnp.maximum(g - 1, 0)
    return pl.pallas_call(
        functools.partial(_layer1_kernel, tiles_per_seq=seq // ts),
        grid=(n + 1,),
        in_specs=[pl.BlockSpec((ts, D_MODEL), lambda g: (jnp.minimum(g, n - 1), 0)),
                  pl.BlockSpec((None, ts, PLE_DIM), lambda g: (layer, previous(g), 0))]
        + [_const_spec(a.shape) for a in mixer_consts] + [_layer_spec(a, layer) for a in ffn_params],
        out_specs=pl.BlockSpec((ts, D_MODEL), lambda g: (previous(g), 0)),
        out_shape=jax.ShapeDtypeStruct((t, D_MODEL), F32),
        scratch_shapes=[
            pltpu.VMEM((HALO + ts, CONV_WIDTH), F32),
            pltpu.VMEM((SUBLANES - 1, ts + HALO - SUBLANES, CONV_WIDTH), F32),
            pltpu.VMEM((ts, CONV_WIDTH), F32),
            pltpu.VMEM((ts, MIX_WIDTH), BF16),
            pltpu.VMEM((ts, SG_WIDTH), F32),
            pltpu.VMEM((ts, SG_WIDTH), BF16),
            pltpu.VMEM((2, ts, D_MODEL), F32),
            pltpu.VMEM((ts, D_FF), BF16),
        ],
        compiler_params=_params(1),
        name="layer1",
    )(x2d, p3d, *mixer_consts, *ffn_params)


def _row(vec):
    return vec.reshape(1, -1)


def kernel(x, p, even_w_in, even_w_out, pool_w, pool_scale, odd_w_in, odd_w_out, conv_dw, conv_ln_g,
           conv_ln_b, sg_ln_g, sg_ln_b, sg_w, sg_b, ln_mix_g, ln_mix_b, ffn_w_gate, ffn_w_up, ffn_w_down,
           ln_ffn_g, ln_ffn_b, ple_w_proj, ple_w_gate, ple_b_gate):
    b, s, d = x.shape
    t = b * s
    bf = lambda w: w.astype(BF16)

    p3d = p.reshape(DEPTH, t, PLE_DIM)
    rows = lambda m: m.reshape(DEPTH, 1, -1)
    ffn_params = (bf(ffn_w_gate), bf(ffn_w_up), bf(ffn_w_down), rows(ln_ffn_g), rows(ln_ffn_b),
                  bf(ple_w_gate), rows(ple_b_gate), bf(ple_w_proj))

    q, k, v, u = _inproj0(x.reshape(t, d), bf(even_w_in[0]))
    attn = _attention(q.reshape(b, s, SB_WIDTH), k.reshape(b, s, SB_WIDTH), v.reshape(b, s, SB_WIDTH))
    x = _mix0(attn, u.reshape(b, s, POOL_WIDTH), x, bf(pool_w[0]), _row(pool_scale[0]), bf(even_w_out[0]),
              _row(ln_mix_g[0]), _row(ln_mix_b[0]))
    x = _ffn(x.reshape(t, d), p3d, 0, *ffn_params)

    sg_bias = jnp.repeat(sg_b[0].T, SG_GROUP_DIM, axis=1)
    return _layer1(x, p3d, 1, s, bf(odd_w_in[0]), _row(sg_ln_g[0]), _row(sg_ln_b[0]), conv_dw[0],
                   _row(conv_ln_g[0]), _row(conv_ln_b[0]), sg_w[0], sg_bias, bf(odd_w_out[0]), _row(ln_mix_g[1]),
                   _row(ln_mix_b[1]), ffn_params).reshape(b, s, d)
```

```python
import functools

import jax
import jax.numpy as jnp
from jax import lax
from jax.experimental import pallas as pl
from jax.experimental.pallas import tpu as pltpu

F32 = jnp.float32
BF16 = jnp.bfloat16

D_MODEL = 1024
N_SB_HEADS = 8
SB_HEAD_DIM = 64
SB_WIDTH = N_SB_HEADS * SB_HEAD_DIM
POOL_WINDOWS = (2, 4, 8, 16)
POOL_GROUP_DIM = 128
POOL_WIDTH = len(POOL_WINDOWS) * POOL_GROUP_DIM
CONV_WIDTH = 512
CONV_TAPS = 31
SG_GROUPS = 4
SG_GROUP_DIM = 128
SG_WIDTH = SG_GROUPS * SG_GROUP_DIM
SG_CHUNK = 128
MIX_WIDTH = 1024
D_FF = 2816
PLE_DIM = 256
DEPTH = 2
ALPHA = (2 * DEPTH) ** 0.25
LN_EPS = 1e-5

LANES = 128
SUBLANES = 8
MXU_DIM = 256
VMEM_LIMIT = 56 * 1024 * 1024

ROW_TILE = 1024
SUB_ROWS = 256
FUSED_TILE = 512
HALO = 32
KEY_BLOCK = 128
Q_TILE = 256
ROW_PART = 64
TILES_PER_TRIP = 4
EXP_UNDERFLOW_LOG = -104.0
FF_CHUNK = MXU_DIM
CONV_ROWS = 64

assert POOL_WINDOWS == tuple(2 ** (g + 1) for g in range(len(POOL_WINDOWS)))
assert HALO == SUBLANES * len(POOL_WINDOWS) and HALO >= CONV_TAPS - 1
assert Q_TILE == 2 * KEY_BLOCK and KEY_BLOCK % ROW_PART == 0


def _layer_norm(y, g, b):
    mu = jnp.mean(y, axis=-1, keepdims=True)
    d = y - mu
    var = jnp.mean(d * d, axis=-1, keepdims=True)
    return d * lax.rsqrt(var + LN_EPS) * g + b


def _dot(a, b):
    return jnp.dot(a, b, preferred_element_type=F32)


def _const_spec(shape):
    zeros = (0,) * len(shape)
    return pl.BlockSpec(shape, lambda *_: zeros, pipeline_mode=pl.Buffered(1))


def _layer_spec(arr, layer):
    index = (layer,) + (0,) * (arr.ndim - 1)
    return pl.BlockSpec((None,) + arr.shape[1:], lambda *_: index, pipeline_mode=pl.Buffered(1))


def _params(n_axes):
    return pltpu.CompilerParams(dimension_semantics=("arbitrary",) * n_axes,
                                vmem_limit_bytes=VMEM_LIMIT)


def _skewed(n_sub, stages):
    values = [None] * n_sub
    for step in range(n_sub + len(stages) - 1):
        for k, stage in enumerate(stages):
            s = step - k
            if 0 <= s < n_sub:
                values[s] = stage(s, values[s])


def _inproj0_kernel(x_ref, w_ref, q_ref, k_ref, v_ref, u_ref):
    w = SB_WIDTH
    for s in range(x_ref.shape[0] // SUB_ROWS):
        rows = slice(s * SUB_ROWS, (s + 1) * SUB_ROWS)
        xb = x_ref[rows, :].astype(BF16)
        q_ref[rows, :] = (_dot(xb, w_ref[:, 0:w]) * (SB_HEAD_DIM ** -0.5)).astype(BF16)
        k_ref[rows, :] = _dot(xb, w_ref[:, w:2 * w]).astype(BF16)
        v_ref[rows, :] = _dot(xb, w_ref[:, 2 * w:3 * w]).astype(BF16)
        u_ref[rows, :] = _dot(xb, w_ref[:, 3 * w:])


def _inproj0(x2d, w_in):
    t = x2d.shape[0]
    row = lambda i: (i, 0)
    return pl.pallas_call(
        _inproj0_kernel,
        grid=(t // ROW_TILE,),
        in_specs=[pl.BlockSpec((ROW_TILE, D_MODEL), row), _const_spec(w_in.shape)],
        out_specs=[pl.BlockSpec((ROW_TILE, SB_WIDTH), row)] * 3 + [pl.BlockSpec((ROW_TILE, POOL_WIDTH), row)],
        out_shape=[jax.ShapeDtypeStruct((t, SB_WIDTH), BF16)] * 3 + [jax.ShapeDtypeStruct((t, POOL_WIDTH), F32)],
        compiler_params=_params(1),
        name="inproj0",
    )(x2d, w_in)


def _attn_kernel(q_ref, k_ref, v_ref, o_ref, kt_ref, vcat_ref, sfx_ref, acc_ref, carry_ref, *, seq):
    nb = seq // KEY_BLOCK
    kb_cols = 2 * KEY_BLOCK
    half = LANES // 2
    per_tile = Q_TILE // KEY_BLOCK

    kt = k_ref[...].astype(F32).T
    dim = lax.broadcasted_iota(jnp.int32, kt.shape, 0)
    kt_a = jnp.where(dim < half, kt, 0.0).astype(BF16)
    kt_b = jnp.where(dim < half, 0.0, kt).astype(BF16)
    for kb in range(nb):
        kt_ref[kb, :, 0:KEY_BLOCK] = kt_a[:, kb * KEY_BLOCK:(kb + 1) * KEY_BLOCK]
        kt_ref[kb, :, KEY_BLOCK:] = kt_b[:, kb * KEY_BLOCK:(kb + 1) * KEY_BLOCK]
    lane = lax.broadcasted_iota(jnp.int32, (seq, LANES), 1)
    v = v_ref[...]
    zero = jnp.zeros_like(v)
    vcat_ref[:, 0:KEY_BLOCK, :] = jnp.where(lane < half, v, zero).reshape(nb, KEY_BLOCK, LANES)
    vcat_ref[:, KEY_BLOCK:, :] = jnp.where(lane < half, zero, v).reshape(nb, KEY_BLOCK, LANES)

    rj = lax.broadcasted_iota(jnp.int32, (kb_cols, kb_cols), 0)
    cs = lax.broadcasted_iota(jnp.int32, (kb_cols, kb_cols), 1)
    in_suffix = ((rj < KEY_BLOCK) == (cs < KEY_BLOCK)) & (rj >= cs)
    sfx_ref[...] = jnp.where(in_suffix, -1.0, 0.0).astype(BF16)
    out_is_a = lax.broadcasted_iota(jnp.int32, (ROW_PART, LANES), 1) < half

    row = lax.broadcasted_iota(jnp.int32, (KEY_BLOCK, kb_cols), 0)
    col = lax.broadcasted_iota(jnp.int32, (KEY_BLOCK, kb_cols), 1)
    on_diagonal = jnp.where(col >= KEY_BLOCK, col - KEY_BLOCK, col) < row
    parts = [slice(r, r + ROW_PART) for r in range(0, Q_TILE, ROW_PART)]

    def mask_diagonal(val):
        top = jnp.where(on_diagonal, val[0:KEY_BLOCK], 0.0)
        return top if val.shape[0] == KEY_BLOCK else jnp.concatenate([top, val[KEY_BLOCK:]], axis=0)

    def visit(jobs):
        flat = [(qb, slot, blk) for qb, slot, blocks in jobs for blk in blocks]
        zs = [_dot(qb[r0:r0 + nr], kt_ref[kb]) for qb, _, (kb, r0, nr, _) in flat]
        sps = []
        for z, (_, _, blk) in zip(zs, flat):
            sp = jnp.maximum(z, 0.0) + jnp.log(1.0 + jnp.exp(-jnp.abs(z)))
            sps.append(mask_diagonal(sp) if blk[3] else sp)
        sums = [_dot(sp.astype(BF16), sfx_ref[...]) for sp in sps]
        pvs = []
        for s, z, (_, _, (kb, _, _, diagonal)) in zip(sums, zs, flat):
            w = jnp.exp(z + s)
            if diagonal:
                w = mask_diagonal(w)
            pvs.append(_dot(w.astype(BF16), vcat_ref[kb]))
        for _, slot, _ in jobs:
            acc = [acc_ref[slot, part, :] for part in parts]
            carry = [carry_ref[slot, part, :] for part in parts]
            for pv, sp, (_, blk_slot, (_, r0, nr, _)) in zip(pvs, sps, flat):
                if blk_slot != slot:
                    continue
                tot_a = jnp.sum(sp[:, 0:KEY_BLOCK], axis=1, keepdims=True)
                tot_b = jnp.sum(sp[:, KEY_BLOCK:], axis=1, keepdims=True)
                for n, part in enumerate(parts):
                    if r0 <= part.start and part.stop <= r0 + nr:
                        mine = slice(part.start - r0, part.stop - r0)
                        acc[n] = acc[n] + jnp.exp(carry[n]) * pv[mine]
                        carry[n] = carry[n] - jnp.where(out_is_a, tot_a[mine], tot_b[mine])
            for n, part in enumerate(parts):
                acc_ref[slot, part, :] = acc[n]
                carry_ref[slot, part, :] = carry[n]

    def first_blocks(i, with_previous):
        kd = i * per_tile
        blocks = [(kd + 1, KEY_BLOCK, KEY_BLOCK, True), (kd, 0, Q_TILE, True)]
        if with_previous:
            blocks += [(kd - 1, 0, KEY_BLOCK + ROW_PART, False), (kd - 2, 0, ROW_PART, False)]
        return blocks

    def walk_on(i, qb, slot):
        def max_carry():
            return jnp.max(carry_ref[slot])

        kd = i * per_tile

        @pl.when((i > 0) & (max_carry() > EXP_UNDERFLOW_LOG))
        def _():
            visit([(qb, slot, [(kd - 1, KEY_BLOCK + ROW_PART, KEY_BLOCK - ROW_PART, False),
                               (kd - 2, ROW_PART, KEY_BLOCK - ROW_PART, False),
                               (kd - 2, KEY_BLOCK, KEY_BLOCK, False)])])

        left = jnp.maximum(i - 1, 0) * per_tile
        trips = left // (2 * per_tile)

        def k_cond(state):
            n, top = state
            return (n < trips) & (top > EXP_UNDERFLOW_LOG)

        def k_body(state):
            n, _ = state
            first = left - 1 - n * 2 * per_tile
            visit([(qb, slot, [(first - d, 0, Q_TILE, False) for d in range(2 * per_tile)])])
            return n + 1, max_carry()

        n_done, top = lax.while_loop(k_cond, k_body, (0, max_carry()))

        @pl.when((n_done == trips) & (left - trips * 2 * per_tile > 0) & (top > EXP_UNDERFLOW_LOG))
        def _():
            visit([(qb, slot, [(per_tile - 1 - d, 0, Q_TILE, False) for d in range(per_tile)])])

    def trip_body(j, _):
        tiles = [TILES_PER_TRIP * j + n for n in range(TILES_PER_TRIP)]
        starts = [pl.multiple_of(i * Q_TILE, Q_TILE) for i in tiles]
        qbs = [q_ref[pl.ds(q0, Q_TILE), :] for q0 in starts]
        acc_ref[...] = jnp.zeros_like(acc_ref)
        carry_ref[...] = jnp.zeros_like(carry_ref)

        @pl.when(j == 0)
        def _():
            visit([(qb, slot, first_blocks(i, slot > 0)) for slot, (i, qb) in enumerate(zip(tiles, qbs))])

        @pl.when(j > 0)
        def _():
            visit([(qb, slot, first_blocks(i, True)) for slot, (i, qb) in enumerate(zip(tiles, qbs))])

        live = jnp.where(j > 0, carry_ref[0], carry_ref[1])
        for slot in range(1, TILES_PER_TRIP):
            live = jnp.maximum(live, carry_ref[slot])

        @pl.when(jnp.max(live) > EXP_UNDERFLOW_LOG)
        def _():
            for slot, (i, qb) in enumerate(zip(tiles, qbs)):
                walk_on(i, qb, slot)

        for slot, q0 in enumerate(starts):
            o_ref[pl.ds(q0, Q_TILE), :] = acc_ref[slot].astype(o_ref.dtype)
        return 0

    lax.fori_loop(0, seq // (TILES_PER_TRIP * Q_TILE), trip_body, 0)


def _attention(q, k, v):
    b, s, _ = q.shape
    nb = s // KEY_BLOCK
    spec = pl.BlockSpec((None, s, LANES), lambda bi, j: (bi, 0, j))
    return pl.pallas_call(
        functools.partial(_attn_kernel, seq=s),
        grid=(b, SB_WIDTH // LANES),
        in_specs=[spec, spec, spec],
        out_specs=spec,
        out_shape=jax.ShapeDtypeStruct((b, s, SB_WIDTH), BF16),
        scratch_shapes=[
            pltpu.VMEM((nb, LANES, 2 * KEY_BLOCK), BF16),
            pltpu.VMEM((nb, 2 * KEY_BLOCK, LANES), BF16),
            pltpu.VMEM((2 * KEY_BLOCK, 2 * KEY_BLOCK), BF16),
            pltpu.VMEM((TILES_PER_TRIP, Q_TILE, LANES), F32),
            pltpu.VMEM((TILES_PER_TRIP, Q_TILE, LANES), F32),
        ],
        compiler_params=_params(2),
        name="stickbreak_attn",
    )(q, k, v)


def _mix0_kernel(a_ref, u_ref, uh_ref, x_ref, pw_ref, ps_ref, wo_ref, g_ref, b_ref, o_ref,
                 ubuf, s2, s4, s8, s16_ref):
    i = pl.program_id(1)
    ts = u_ref.shape[0]
    n = HALO + ts
    gd = POOL_GROUP_DIM
    ubuf[0:HALO, :] = jnp.where(i > 0, uh_ref[...], 0.0)
    ubuf[HALO:, :] = u_ref[...]
    window_sums = (s2, s4, s8, s16_ref)
    shorter = ubuf
    for g, (w, sums) in enumerate(zip(POOL_WINDOWS, window_sums)):
        first, back, cols = SUBLANES * (g + 1), w // 2, slice(g * gd, None)
        sums[first:n, cols] = shorter[first:n, cols] + shorter[first - back:n - back, cols]
        shorter = sums

    def project(s, _):
        rows = slice(s * SUB_ROWS, (s + 1) * SUB_ROWS)
        held = slice(HALO + s * SUB_ROWS, HALO + (s + 1) * SUB_ROWS)
        pos1 = (i * ts + s * SUB_ROWS + 1
                + lax.broadcasted_iota(jnp.int32, (SUB_ROWS, gd), 0)).astype(F32)
        mixed = []
        for g, w in enumerate(POOL_WINDOWS):
            cols = slice(g * gd, (g + 1) * gd)
            pooled = window_sums[g][held, cols] / jnp.minimum(pos1, float(w)) - u_ref[rows, cols]
            mixed.append((_dot(pooled.astype(BF16), pw_ref[g]) * ps_ref[:, cols]).astype(BF16))
        cat = jnp.concatenate([a_ref[rows, :]] + mixed, axis=1)
        return _dot(cat, wo_ref[...])

    def normalize(s, mix):
        rows = slice(s * SUB_ROWS, (s + 1) * SUB_ROWS)
        o_ref[rows, :] = _layer_norm(ALPHA * x_ref[rows, :] + mix, g_ref[...], b_ref[...])

    _skewed(ts // SUB_ROWS, [project, normalize])


def _mix0(a, u, x, pool_w, pool_scale, w_out, ln_g, ln_b):
    b, s, _ = x.shape
    ts = ROW_TILE
    tile = lambda bi, i: (bi, i, 0)
    halo = lambda bi, i: (bi, jnp.maximum(i * (ts // HALO) - 1, 0), 0)
    return pl.pallas_call(
        _mix0_kernel,
        grid=(b, s // ts),
        in_specs=[
            pl.BlockSpec((None, ts, SB_WIDTH), tile),
            pl.BlockSpec((None, ts, POOL_WIDTH), tile),
            pl.BlockSpec((None, HALO, POOL_WIDTH), halo),
            pl.BlockSpec((None, ts, D_MODEL), tile),
            _const_spec(pool_w.shape), _const_spec(pool_scale.shape), _const_spec(w_out.shape),
            _const_spec(ln_g.shape), _const_spec(ln_b.shape),
        ],
        out_specs=pl.BlockSpec((None, ts, D_MODEL), tile),
        out_shape=jax.ShapeDtypeStruct((b, s, D_MODEL), F32),
        scratch_shapes=[pltpu.VMEM((HALO + ts, POOL_WIDTH), F32)] * 5,
        compiler_params=_params(2),
        name="mix0",
    )(a, u, u, x, pool_w, pool_scale, w_out, ln_g, ln_b)


def _ffn_kernel(x_ref, p_ref, wg_ref, wu_ref, wd_ref, g_ref, b_ref, wpg_ref, bpg_ref, wpp_ref, o_ref,
                h_ref):
    def rows(s):
        return slice(s * SUB_ROWS, (s + 1) * SUB_ROWS)

    def swiglu(s, _):
        xb = x_ref[rows(s), :].astype(BF16)
        for c in range(D_FF // FF_CHUNK):
            cols = slice(c * FF_CHUNK, (c + 1) * FF_CHUNK)
            gate = _dot(xb, wg_ref[:, cols])
            up = _dot(xb, wu_ref[:, cols])
            h_ref[rows(s), cols] = (jax.nn.silu(gate) * up).astype(BF16)
        return _dot(h_ref[rows(s), :], wd_ref[...])

    def norm_and_embed(s, y):
        x2 = _layer_norm(ALPHA * x_ref[rows(s), :] + y, g_ref[...], b_ref[...])
        ple_gate = jax.nn.sigmoid(_dot(x2.astype(BF16), wpg_ref[...]) + bpg_ref[...])
        o_ref[rows(s), :] = x2 + ple_gate * _dot(p_ref[rows(s), :].astype(BF16), wpp_ref[...])

    _skewed(x_ref.shape[0] // SUB_ROWS, [swiglu, norm_and_embed])


def _ffn(x2d, p3d, layer, w_gate, w_up, w_down, ln_g, ln_b, w_pg, b_pg, w_pp):
    t = x2d.shape[0]
    row = lambda i: (i, 0)
    consts = (w_gate, w_up, w_down, ln_g, ln_b, w_pg, b_pg, w_pp)

    return pl.pallas_call(
        _ffn_kernel,
        grid=(t // ROW_TILE,),
        in_specs=[pl.BlockSpec((ROW_TILE, D_MODEL), row),
                  pl.BlockSpec((None, ROW_TILE, PLE_DIM), lambda i: (layer, i, 0))]
        + [_layer_spec(c, layer) for c in consts],
        out_specs=pl.BlockSpec((ROW_TILE, D_MODEL), row),
        out_shape=jax.ShapeDtypeStruct((t, D_MODEL), F32),
        scratch_shapes=[pltpu.VMEM((ROW_TILE, D_FF), BF16)],
        compiler_params=_params(1),
        name="ffn_ple",
    )(x2d, p3d, *consts)


def _merge(primary, secondary):
    merged, j = [], 0
    for k, item in enumerate(primary):
        merged.append(item)
        while j < len(secondary) and (j + 1) * len(primary) <= (k + 1) * len(secondary):
            merged.append(secondary[j])
            j += 1
    return merged + secondary[j:]


def _channel_mixer_items(x1_ref, slot, p_ref, o_ref, h_ref, wg_ref, wu_ref, wd_ref, g_ref, b_ref, wpg_ref,
                         bpg_ref, wpp_ref):
    held = {}

    def load_x1():
        held["x1b"] = x1_ref[slot].astype(BF16)

    def swiglu_chunk(k):
        def run():
            cols = slice(k * FF_CHUNK, (k + 1) * FF_CHUNK)
            gate = _dot(held["x1b"], wg_ref[:, cols])
            up = _dot(held["x1b"], wu_ref[:, cols])
            h_ref[:, cols] = (jax.nn.silu(gate) * up).astype(BF16)
        return run

    def project_down():
        held["y"] = _dot(h_ref[...], wd_ref[...])

    def norm_and_embed():
        x2 = _layer_norm(ALPHA * x1_ref[slot] + held.pop("y"), g_ref[...], b_ref[...])
        ple_gate = jax.nn.sigmoid(_dot(x2.astype(BF16), wpg_ref[...]) + bpg_ref[...])
        o_ref[...] = x2 + ple_gate * _dot(p_ref[...].astype(BF16), wpp_ref[...])

    return [load_x1] + [swiglu_chunk(k) for k in range(D_FF // FF_CHUNK)] + [project_down, norm_and_embed]


def _layer1_kernel(x_ref, p_ref, wi_ref, sg_g_ref, sg_b_ref, dw_ref, cg_ref, cb_ref, sw_ref, sb_ref, wo_ref,
                   g1_ref, b1_ref, wg_ref, wu_ref, wd_ref, g2_ref, b2_ref, wpg_ref, bpg_ref, wpp_ref,
                   o_ref, hc_ref, shift_ref, conv_ref, cat_ref, u_ref, v_ref, x1_ref, h_ref, *, tiles_per_seq):
    step = pl.program_id(0)
    ts = x_ref.shape[0]
    c = CONV_WIDTH
    span = ts + HALO - SUBLANES

    @pl.when(step % tiles_per_seq == 0)
    def _():
        hc_ref[0:HALO, :] = jnp.zeros((HALO, c), F32)

    @pl.when(step == 0)
    def _():
        x1_ref[1] = jnp.zeros(x1_ref.shape[1:], F32)

    fill = step % 2
    drain = 1 - fill
    held = {}

    def project_in():
        xb = x_ref[...].astype(BF16)
        held["in"] = [_dot(xb, wi_ref[:, lo:lo + c]) for lo in (0, c, 2 * c, 2 * c + SG_WIDTH)]

    def activate():
        a, gl, zu, zv = held.pop("in")
        hc_ref[HALO:HALO + ts, :] = a * jax.nn.sigmoid(gl)
        u_ref[...] = jax.nn.gelu(zu)
        v_ref[...] = _layer_norm(jax.nn.gelu(zv), sg_g_ref[...], sg_b_ref[...]).astype(BF16)

    def shift_copies():
        for r in range(1, SUBLANES):
            shift_ref[r - 1, 0:span, :] = hc_ref[r:r + span, :]

    first = HALO - (CONV_TAPS - 1)

    def conv_piece(lb, rb):
        def run():
            cols = slice(lb * LANES, (lb + 1) * LANES)
            acc = jnp.zeros((CONV_ROWS, LANES), F32)
            for kk in range(CONV_TAPS):
                r, r0 = (first + kk) % SUBLANES, rb * CONV_ROWS + (first + kk) // SUBLANES * SUBLANES
                taps = (hc_ref[r0:r0 + CONV_ROWS, cols] if r == 0 else shift_ref[r - 1, r0:r0 + CONV_ROWS, cols])
                acc = acc + dw_ref[kk:kk + 1, cols] * taps
            conv_ref[rb * CONV_ROWS:(rb + 1) * CONV_ROWS, cols] = acc
        return run

    def conv_norm():
        cat_ref[:, 0:c] = jax.nn.silu(_layer_norm(conv_ref[...], cg_ref[...], cb_ref[...])).astype(BF16)

    tt = lax.broadcasted_iota(jnp.int32, (SG_CHUNK, SG_CHUNK), 0)
    ss = lax.broadcasted_iota(jnp.int32, (SG_CHUNK, SG_CHUNK), 1)

    def gating(g):
        def run():
            cols = slice(g * SG_GROUP_DIM, (g + 1) * SG_GROUP_DIM)
            chunks = [slice(k * SG_CHUNK, (k + 1) * SG_CHUNK) for k in range(ts // SG_CHUNK)]
            gate_w = jnp.where(ss <= tt, sw_ref[g], 0.0).astype(BF16)
            sv = _dot(gate_w, jnp.concatenate([v_ref[ch, cols] for ch in chunks], axis=1))
            for k, ch in enumerate(chunks):
                gated = u_ref[ch, cols] * (sv[:, k * SG_GROUP_DIM:(k + 1) * SG_GROUP_DIM] + sb_ref[:, cols])
                cat_ref[ch, c + g * SG_GROUP_DIM:c + (g + 1) * SG_GROUP_DIM] = gated.astype(BF16)
        return run

    def project_out():
        mix = _dot(cat_ref[...], wo_ref[...])
        x1_ref[fill] = _layer_norm(ALPHA * x_ref[...] + mix, g1_ref[...], b1_ref[...])
        hc_ref[0:HALO, :] = hc_ref[ts:ts + HALO, :]

    token_items = ([project_in, activate, shift_copies]
                   + [conv_piece(lb, rb) for lb in range(c // LANES) for rb in range(ts // CONV_ROWS)]
                   + [conv_norm] + [gating(g) for g in range(SG_GROUPS)] + [project_out])

    channel_items = _channel_mixer_items(x1_ref, drain, p_ref, o_ref, h_ref, wg_ref, wu_ref, wd_ref, g2_ref,
                                         b2_ref, wpg_ref, bpg_ref, wpp_ref)
    for item in token_items[:3] + _merge(channel_items, token_items[3:]):
        item()


def _layer1(x2d, p3d, layer, seq, w_in, sg_ln_g, sg_ln_b, conv_dw, conv_g, conv_b, sg_w, sg_bias, w_out,
            ln_g, ln_b, ffn_params):
    t = x2d.shape[0]
    ts = FUSED_TILE
    n = t // ts
    mixer_consts = (w_in, sg_ln_g, sg_ln_b, conv_dw, conv_g, conv_b, sg_w, sg_bias, w_out, ln_g, ln_b)
    previous = lambda g: jnp.maximum(g - 1, 0)
    return pl.pallas_call(
        functools.partial(_layer1_kernel, tiles_per_seq=seq // ts),
        grid=(n + 1,),
        in_specs=[pl.BlockSpec((ts, D_MODEL), lambda g: (jnp.minimum(g, n - 1), 0)),
                  pl.BlockSpec((None, ts, PLE_DIM), lambda g: (layer, previous(g), 0))]
        + [_const_spec(a.shape) for a in mixer_consts] + [_layer_spec(a, layer) for a in ffn_params],
        out_specs=pl.BlockSpec((ts, D_MODEL), lambda g: (previous(g), 0)),
        out_shape=jax.ShapeDtypeStruct((t, D_MODEL), F32),
        scratch_shapes=[
            pltpu.VMEM((HALO + ts, CONV_WIDTH), F32),
            pltpu.VMEM((SUBLANES - 1, ts + HALO - SUBLANES, CONV_WIDTH), F32),
            pltpu.VMEM((ts, CONV_WIDTH), F32),
            pltpu.VMEM((ts, MIX_WIDTH), BF16),
            pltpu.VMEM((ts, SG_WIDTH), F32),
            pltpu.VMEM((ts, SG_WIDTH), BF16),
            pltpu.VMEM((2, ts, D_MODEL), F32),
            pltpu.VMEM((ts, D_FF), BF16),
        ],
        compiler_params=_params(1),
        name="layer1",
    )(x2d, p3d, *mixer_consts, *ffn_params)


def _row(vec):
    return vec.reshape(1, -1)


def kernel(x, p, even_w_in, even_w_out, pool_w, pool_scale, odd_w_in, odd_w_out, conv_dw, conv_ln_g,
           conv_ln_b, sg_ln_g, sg_ln_b, sg_w, sg_b, ln_mix_g, ln_mix_b, ffn_w_gate, ffn_w_up, ffn_w_down,
           ln_ffn_g, ln_ffn_b, ple_w_proj, ple_w_gate, ple_b_gate):
    b, s, d = x.shape
    t = b * s
    bf = lambda w: w.astype(BF16)

    p3d = p.reshape(DEPTH, t, PLE_DIM)
    rows = lambda m: m.reshape(DEPTH, 1, -1)
    ffn_params = (bf(ffn_w_gate), bf(ffn_w_up), bf(ffn_w_down), rows(ln_ffn_g), rows(ln_ffn_b),
                  bf(ple_w_gate), rows(ple_b_gate), bf(ple_w_proj))

    q, k, v, u = _inproj0(x.reshape(t, d), bf(even_w_in[0]))
    attn = _attention(q.reshape(b, s, SB_WIDTH), k.reshape(b, s, SB_WIDTH), v.reshape(b, s, SB_WIDTH))
    x = _mix0(attn, u.reshape(b, s, POOL_WIDTH), x, bf(pool_w[0]), _row(pool_scale[0]), bf(even_w_out[0]),
              _row(ln_mix_g[0]), _row(ln_mix_b[0]))
    x = _ffn(x.reshape(t, d), p3d, 0, *ffn_params)

    sg_bias = jnp.repeat(sg_b[0].T, SG_GROUP_DIM, axis=1)
    return _layer1(x, p3d, 1, s, bf(odd_w_in[0]), _row(sg_ln_g[0]), _row(sg_ln_b[0]), conv_dw[0],
                   _row(conv_ln_g[0]), _row(conv_ln_b[0]), sg_w[0], sg_bias, bf(odd_w_out[0]), _row(ln_mix_g[1]),
                   _row(ln_mix_b[1]), ffn_params).reshape(b, s, d)
```
